```python
import numpy as np
import jax
import jax.numpy as jnp
from jax import lax

D_MODEL = 2048
BATCH = 2
SEQ = 8192
DEPTH = 1

HEAD_DIM_A = 64
HQ_A = 16
HKV_A = 4
GQA_GROUP = HQ_A // HKV_A
WINDOW = 128
ROPE_THETA = 10000.0
H_B = 4
DK_B = D_MODEL // 2 // H_B
DV_B = D_MODEL // H_B
GK_RANK = 16
GK_NORMALIZER = 16.0
CHUNK = 64
D_FF = ((8 * D_MODEL // 3 + 255) // 256) * 256
CONV_WIDTH = 3
EPS = 1e-6

Q_A = HQ_A * HEAD_DIM_A
KV_A = HKV_A * HEAD_DIM_A
QK_B = H_B * DK_B
V_B = H_B * DV_B
SPLIT_SIZES = (Q_A, KV_A, KV_A, QK_B, QK_B, V_B, GK_RANK, V_B, D_MODEL, D_MODEL)
IN_COLS = sum(SPLIT_SIZES)

kernel_name = "hybrid_swa_gla_convffn_block"


def _rms_norm(x, w):
    xf = x.astype(jnp.float32)
    y = xf * lax.rsqrt(jnp.mean(xf * xf, axis=-1, keepdims=True) + EPS)
    return (y * w.astype(jnp.float32)).astype(x.dtype)


def _rope_tables(positions):
    inv_freq = ROPE_THETA ** (-jnp.arange(0, HEAD_DIM_A, 2, dtype=jnp.float32) / HEAD_DIM_A)
    ang = positions.astype(jnp.float32)[..., None] * inv_freq
    return jnp.cos(ang)[:, :, None, :], jnp.sin(ang)[:, :, None, :]


def _apply_rope(x, cos, sin):
    xf = x.astype(jnp.float32)
    x1, x2 = jnp.split(xf, 2, axis=-1)
    return jnp.concatenate([x1 * cos - x2 * sin, x2 * cos + x1 * sin], axis=-1).astype(x.dtype)


def _sliding_window_attention(q, k, v, sinks):
    B, S = q.shape[0], q.shape[1]
    nb = S // WINDOW
    qb = q.reshape(B, nb, WINDOW, HKV_A, GQA_GROUP, HEAD_DIM_A)

    def band(t):
        tp = jnp.pad(t, ((0, 0), (WINDOW, 0), (0, 0), (0, 0)))
        tp = tp.reshape(B, nb + 1, WINDOW, HKV_A, HEAD_DIM_A)
        return jnp.concatenate([tp[:, :-1], tp[:, 1:]], axis=2)

    kb, vb = band(k), band(v)
    s = jnp.einsum("bnqhgd,bnkhd->bnhgqk", qb, kb).astype(jnp.float32) * (HEAD_DIM_A ** -0.5)
    i = jnp.arange(WINDOW)[:, None]
    j = jnp.arange(2 * WINDOW)[None, :]
    dist = WINDOW + i - j
    in_win = (dist >= 0) & (dist < WINDOW)
    blk = jnp.arange(nb)[:, None, None]
    valid = in_win[None] & ((blk > 0) | (j[None] >= WINDOW))
    s = jnp.where(valid[None, :, None, None], s, -jnp.inf)
    sink = sinks.astype(jnp.float32).reshape(1, 1, HKV_A, GQA_GROUP, 1, 1)
    m = jnp.maximum(jnp.max(s, axis=-1, keepdims=True), sink)
    p = jnp.exp(s - m)
    denom = jnp.sum(p, axis=-1, keepdims=True) + jnp.exp(sink - m)
    probs = (p / denom).astype(v.dtype)
    o = jnp.einsum("bnhgqk,bnkhd->bnqhgd", probs, vb)
    return o.reshape(B, S, Q_A)


def _gla_chunked(q, k, v, g):
    B, S = q.shape[0], q.shape[1]
    nc = S // CHUNK

    def chunks(t):
        return t.reshape(B, nc, CHUNK, H_B, t.shape[-1])

    q, k, v, g = chunks(q), chunks(k), chunks(v), chunks(g)
    b = jnp.cumsum(g, axis=2)
    b_mid = b[:, :, CHUNK // 2:CHUNK // 2 + 1]
    b_last = b[:, :, CHUNK - 1:CHUNK]
    a = jnp.einsum("bnihd,bnjhd->bnhij", q * jnp.exp(b - b_mid), k * jnp.exp(b_mid - b))
    a = jnp.where(jnp.tril(jnp.ones((CHUNK, CHUNK), dtype=bool)), a, 0.0)
    o_intra = jnp.einsum("bnhij,bnjhe->bnihe", a, v)
    q_in = q * jnp.exp(b)
    k_st = k * jnp.exp(b_last - b)
    decay = jnp.exp(b_last[:, :, 0])

    def step(state, xs):
        qn, kn, vn, dn = xs
        o = jnp.einsum("bihd,bhde->bihe", qn, state)
        state = state * dn[..., None] + jnp.einsum("bjhd,bjhe->bhde", kn, vn)
        return state, o

    xs = tuple(jnp.moveaxis(t, 1, 0) for t in (q_in, k_st, v, decay))
    state0 = jnp.zeros((B, H_B, DK_B, DV_B), jnp.float32)
    _, o_inter = lax.scan(step, state0, xs)
    o = o_intra + jnp.moveaxis(o_inter, 0, 1)
    return o.reshape(B, S, H_B, DV_B)


def _token_mixer(h, cos, sin, w_in, sinks, w_gk_up, b_gk, gla_norm, w_br_a, w_br_b, w_out):
    B, S = h.shape[0], h.shape[1]
    proj = h @ w_in
    points = [int(p) for p in np.cumsum(SPLIT_SIZES)[:-1]]
    q_a, k_a, v_a, q_b, k_b, v_b, gk_lr, og_b, g_a, g_b = jnp.split(proj, points, axis=-1)
    q_a = _apply_rope(q_a.reshape(B, S, HQ_A, HEAD_DIM_A), cos, sin)
    k_a = _apply_rope(k_a.reshape(B, S, HKV_A, HEAD_DIM_A), cos, sin)
    v_a = v_a.reshape(B, S, HKV_A, HEAD_DIM_A)
    o_a = _sliding_window_attention(q_a, k_a, v_a, sinks)
    gk = jax.nn.log_sigmoid((gk_lr @ w_gk_up + b_gk).astype(jnp.float32)) / GK_NORMALIZER
    o_b = _gla_chunked(
        q_b.astype(jnp.float32).reshape(B, S, H_B, DK_B) * (DK_B ** -0.5),
        k_b.astype(jnp.float32).reshape(B, S, H_B, DK_B),
        v_b.astype(jnp.float32).reshape(B, S, H_B, DV_B),
        gk.reshape(B, S, H_B, DK_B))
    o_b = _rms_norm(o_b, gla_norm) * jax.nn.silu(og_b.astype(jnp.float32).reshape(B, S, H_B, DV_B))
    o_b = o_b.astype(h.dtype).reshape(B, S, V_B)
    merged = jax.nn.sigmoid(g_a) * (o_a @ w_br_a) + jax.nn.sigmoid(g_b) * (o_b @ w_br_b)
    return merged @ w_out


def _conv_ffn(h, w_up, conv_w, conv_b, w_down):
    u = h @ w_up
    u = lax.conv_general_dilated(
        u, conv_w[:, None, :], window_strides=(1,), padding=[(CONV_WIDTH - 1, 0)],
        dimension_numbers=("NWC", "WIO", "NWC"), feature_group_count=2 * D_FF) + conv_b
    gate, val = jnp.split(u, 2, axis=-1)
    return (jax.nn.silu(gate) * val) @ w_down


def setup_inputs(seed: int = 0) -> dict:
    key = jax.random.key(seed)
    ks = jax.random.split(key, 24)

    def nrm(k, shape, scale):
        return jax.random.normal(k, shape, jnp.float32) * scale

    L, D = DEPTH, D_MODEL
    return {
        "x": nrm(ks[0], (BATCH, SEQ, D), 1.0),
        "c": nrm(ks[1], (BATCH, D), 1.0),
        "positions": jnp.broadcast_to(jnp.arange(SEQ, dtype=jnp.int32), (BATCH, SEQ)),
        "w_mod": nrm(ks[2], (L, D, 6 * D), 0.5 * D ** -0.5),
        "b_mod": nrm(ks[3], (L, 6 * D), 0.02),
        "mix_norm_pre": 1.0 + nrm(ks[4], (L, D), 0.02),
        "mix_norm_post": 1.0 + nrm(ks[5], (L, D), 0.02),
        "w_in": nrm(ks[6], (L, D, IN_COLS), D ** -0.5),
        "attn_sinks": nrm(ks[7], (L, HQ_A), 0.5),
        "w_gk_up": nrm(ks[8], (L, GK_RANK, QK_B), GK_RANK ** -0.5),
        "b_gk": nrm(ks[9], (L, QK_B), 0.1),
        "gla_norm": 1.0 + nrm(ks[10], (L, DV_B), 0.02),
        "w_branch_attn": nrm(ks[11], (L, Q_A, D), Q_A ** -0.5),
        "w_branch_gla": nrm(ks[12], (L, V_B, D), V_B ** -0.5),
        "w_out": nrm(ks[13], (L, D, D), D ** -0.5),
        "ffn_norm_pre": 1.0 + nrm(ks[14], (L, D), 0.02),
        "ffn_norm_post": 1.0 + nrm(ks[15], (L, D), 0.02),
        "w_up": nrm(ks[16], (L, D, 2 * D_FF), D ** -0.5),
        "conv_w": nrm(ks[17], (L, CONV_WIDTH, 2 * D_FF), CONV_WIDTH ** -0.5),
        "conv_b": nrm(ks[18], (L, 2 * D_FF), 0.02),
        "w_down": nrm(ks[19], (L, D_FF, D), D_FF ** -0.5),
    }


def reference(x, c, positions, w_mod, b_mod, mix_norm_pre, mix_norm_post, w_in, attn_sinks,
              w_gk_up, b_gk, gla_norm, w_branch_attn, w_branch_gla, w_out,
              ffn_norm_pre, ffn_norm_post, w_up, conv_w, conv_b, w_down):
    cos, sin = _rope_tables(positions)
    c_act = jax.nn.silu(c)
    for l in range(DEPTH):
        mod = c_act @ w_mod[l] + b_mod[l]
        sh1, sc1, g1, sh2, sc2, g2 = jnp.split(mod, 6, axis=-1)
        h = _rms_norm(x, mix_norm_pre[l]) * (1.0 + sc1[:, None]) + sh1[:, None]
        y = _token_mixer(h, cos, sin, w_in[l], attn_sinks[l], w_gk_up[l], b_gk[l], gla_norm[l],
                         w_branch_attn[l], w_branch_gla[l], w_out[l])
        x = x + g1[:, None] * _rms_norm(y, mix_norm_post[l])
        h = _rms_norm(x, ffn_norm_pre[l]) * (1.0 + sc2[:, None]) + sh2[:, None]
        y = _conv_ffn(h, w_up[l], conv_w[l], conv_b[l], w_down[l])
        x = x + g2[:, None] * _rms_norm(y, ffn_norm_post[l])
    return x
```

```python
import functools

import jax
import jax.numpy as jnp
from jax import lax
from jax.experimental import pallas as pl
from jax.experimental.pallas import tpu as pltpu

F32 = jnp.float32
BF16 = jnp.bfloat16

D_MODEL = 2048
HEAD_DIM_A = 64
HQ_A = 16
HKV_A = 4
WINDOW = 128
ROPE_THETA = 10000.0
H_B = 4
DK_B = 256
DV_B = 512
GK_RANK = 16
GK_NORMALIZER = 16.0
CHUNK = 64
D_FF = 5632
CONV_WIDTH = 3
EPS = 1e-6

Q_A = HQ_A * HEAD_DIM_A
KV_A = HKV_A * HEAD_DIM_A
QK_B = H_B * DK_B
V_B = H_B * DV_B
QKV_A = Q_A + 2 * KV_A
ROPE_COLS = Q_A + KV_A

LANES = 128
SUBLANES = 8
GK_PAD = LANES
VMEM_LIMIT = 48 * 1024 * 1024

TM_NORM = 512
TM_ROPE = 2048
TM_PROJ_A = 512
TM_MM, TN_MM = 1024, 1024
TM_MERGE, TN_MERGE = 1024, 512
TM_OUT = 512
TM_UP, TN_UP = 1024, 512
TM_DOWN, TK_DOWN = 512, 512
TN_MOD = 1024


def _params(*sem):
    return pltpu.CompilerParams(dimension_semantics=sem, vmem_limit_bytes=VMEM_LIMIT)


def _dot(a, b):
    return jnp.dot(a, b, preferred_element_type=F32)


def _dot_nt(a, b):
    return lax.dot_general(a, b, (((1,), (1,)), ((), ())), preferred_element_type=F32)


def _dot_tn(a, b):
    return lax.dot_general(a, b, (((0,), (0,)), ((), ())), preferred_element_type=F32)


def _mod_kernel(c_ref, w_ref, b_ref, o_ref):
    c = c_ref[...]
    c_act = (c * jax.nn.sigmoid(c)).astype(BF16)
    o_ref[...] = _dot(c_act, w_ref[...].astype(BF16)) + b_ref[...]


def _modulation(c_pad, w_mod, b_mod):
    rows, d = c_pad.shape
    n = w_mod.shape[1]
    return pl.pallas_call(
        _mod_kernel,
        out_shape=jax.ShapeDtypeStruct((rows, n), F32),
        grid=(n // TN_MOD,),
        in_specs=[
            pl.BlockSpec((rows, d), lambda j: (0, 0)),
            pl.BlockSpec((d, TN_MOD), lambda j: (0, j)),
            pl.BlockSpec((1, TN_MOD), lambda j: (0, j)),
        ],
        out_specs=pl.BlockSpec((rows, TN_MOD), lambda j: (0, j)),
        compiler_params=_params("parallel"),
        name="mod",
    )(c_pad, w_mod, b_mod)


def _norm_mod_kernel(x_ref, nw_ref, sc_ref, sh_ref, o_ref):
    x = x_ref[...]
    ms = jnp.mean(x * x, axis=-1, keepdims=True)
    y = x * lax.rsqrt(ms + EPS) * nw_ref[...]
    o_ref[...] = (y * (1.0 + sc_ref[...]) + sh_ref[...]).astype(o_ref.dtype)


def _norm_mod(x2, nw, sc, sh, seq):
    t, d = x2.shape
    per_seq = seq // TM_NORM
    return pl.pallas_call(
        _norm_mod_kernel,
        out_shape=jax.ShapeDtypeStruct((t, d), BF16),
        grid=(t // TM_NORM,),
        in_specs=[
            pl.BlockSpec((TM_NORM, d), lambda i: (i, 0)),
            pl.BlockSpec((1, d), lambda i: (0, 0)),
            pl.BlockSpec((None, 1, d), lambda i: (i // per_seq, 0, 0)),
            pl.BlockSpec((None, 1, d), lambda i: (i // per_seq, 0, 0)),
        ],
        out_specs=pl.BlockSpec((TM_NORM, d), lambda i: (i, 0)),
        compiler_params=_params("parallel"),
        name="norm_mod",
    )(x2, nw, sc, sh)


def _rope_table_kernel(pos_ref, invf_ref, sgn_ref, cos_ref, sin_ref):
    ang = pos_ref[...].astype(F32) * invf_ref[...]
    cos_ref[...] = jnp.cos(ang)
    sin_ref[...] = jnp.sin(ang) * sgn_ref[...]


def _rope_tables(pos_col, invf_row, sgn_row):
    t = pos_col.shape[0]
    return pl.pallas_call(
        _rope_table_kernel,
        out_shape=(jax.ShapeDtypeStruct((t, LANES), F32), jax.ShapeDtypeStruct((t, LANES), F32)),
        grid=(t // TM_ROPE,),
        in_specs=[
            pl.BlockSpec((TM_ROPE, 1), lambda i: (i, 0)),
            pl.BlockSpec((1, LANES), lambda i: (0, 0)),
            pl.BlockSpec((1, LANES), lambda i: (0, 0)),
        ],
        out_specs=(
            pl.BlockSpec((TM_ROPE, LANES), lambda i: (i, 0)),
            pl.BlockSpec((TM_ROPE, LANES), lambda i: (i, 0)),
        ),
        compiler_params=_params("parallel"),
        name="rope_tables",
    )(pos_col, invf_row, sgn_row)


def _proj_a_kernel(h_ref, w_ref, cos_ref, sin_ref, qkv_ref, lr_ref):
    acc = _dot(h_ref[...], w_ref[...])
    cos = cos_ref[...]
    sin = sin_ref[...]
    lane = lax.broadcasted_iota(jnp.int32, cos.shape, 1)
    first_half = (lane % HEAD_DIM_A) < (HEAD_DIM_A // 2)
    half = HEAD_DIM_A // 2
    for c in range(ROPE_COLS // LANES):
        y = acc[:, c * LANES:(c + 1) * LANES]
        partner = jnp.where(first_half, pltpu.roll(y, LANES - half, 1), pltpu.roll(y, half, 1))
        r = y * cos + partner * sin
        if c < Q_A // LANES:
            r = r * (HEAD_DIM_A ** -0.5)
        qkv_ref[:, c * LANES:(c + 1) * LANES] = r.astype(qkv_ref.dtype)
    qkv_ref[:, ROPE_COLS:QKV_A] = acc[:, ROPE_COLS:QKV_A].astype(qkv_ref.dtype)
    lr_ref[...] = acc[:, QKV_A:QKV_A + GK_PAD].astype(lr_ref.dtype)


def _proj_a(h, w_a, cos_t, sin_t):
    t, d = h.shape
    n = w_a.shape[1]
    tm = TM_PROJ_A
    return pl.pallas_call(
        _proj_a_kernel,
        out_shape=(jax.ShapeDtypeStruct((t, QKV_A), BF16), jax.ShapeDtypeStruct((t, GK_PAD), BF16)),
        grid=(t // tm,),
        in_specs=[
            pl.BlockSpec((tm, d), lambda i: (i, 0)),
            pl.BlockSpec((d, n), lambda i: (0, 0)),
            pl.BlockSpec((tm, LANES), lambda i: (i, 0)),
            pl.BlockSpec((tm, LANES), lambda i: (i, 0)),
        ],
        out_specs=(
            pl.BlockSpec((tm, QKV_A), lambda i: (i, 0)),
            pl.BlockSpec((tm, GK_PAD), lambda i: (i, 0)),
        ),
        compiler_params=_params("parallel"),
        name="proj_a",
    )(h, w_a, cos_t, sin_t)


def _mm_kernel(a_ref, w_ref, o_ref, *, act):
    acc = _dot(a_ref[...], w_ref[...])
    if act == "sigmoid":
        acc = jax.nn.sigmoid(acc)
    o_ref[...] = acc.astype(o_ref.dtype)


def _matmul(a, w, *, act, name):
    m, k = a.shape
    n = w.shape[1]
    return pl.pallas_call(
        functools.partial(_mm_kernel, act=act),
        out_shape=jax.ShapeDtypeStruct((m, n), BF16),
        grid=(m // TM_MM, n // TN_MM),
        in_specs=[
            pl.BlockSpec((TM_MM, k), lambda i, j: (i, 0)),
            pl.BlockSpec((k, TN_MM), lambda i, j: (0, j)),
        ],
        out_specs=pl.BlockSpec((TM_MM, TN_MM), lambda i, j: (i, j)),
        compiler_params=_params("parallel", "parallel"),
        name=name,
    )(a, w)


def _swa_kernel(sink_ref, q_ref, kc_ref, kp_ref, vc_ref, vp_ref, o_ref, *, blocks_per_seq):
    w = WINDOW
    no_prev = jnp.where((pl.program_id(0) % blocks_per_seq) > 0, 0, 2 * w)
    row = lax.broadcasted_iota(jnp.int32, (w, 2 * w), 0)
    col = lax.broadcasted_iota(jnp.int32, (w, 2 * w), 1)
    lower = jnp.where(col < w, row + no_prev, -1)
    valid = (col > lower) & (col <= row + w)
    lo = lax.broadcasted_iota(jnp.int32, (2 * w, LANES), 1) < HEAD_DIM_A

    def roll_half(x):
        return pltpu.roll(x.astype(F32), HEAD_DIM_A, 1).astype(x.dtype)

    def softmax(s, sink):
        s = jnp.where(valid, s, -jnp.inf)
        m = jnp.maximum(jnp.max(s, axis=-1, keepdims=True), sink)
        p = jnp.exp(s - m)
        denom = jnp.sum(p, axis=-1, keepdims=True) + jnp.exp(sink - m)
        return p * (1.0 / denom)

    zero = jnp.zeros((2 * w, LANES), BF16)
    for pair in range(HKV_A // 2):
        ls = slice(pair * LANES, (pair + 1) * LANES)
        kt = jnp.concatenate([kp_ref[:, ls], kc_ref[:, ls]], axis=0)
        vt = jnp.concatenate([vp_ref[:, ls], vc_ref[:, ls]], axis=0)
        kt_r, vt_r = roll_half(kt), roll_half(vt)
        for e in range(2):
            hkv = 2 * pair + e
            if e == 0:
                k_bd = jnp.concatenate([jnp.where(lo, kt, zero), jnp.where(lo, zero, kt_r)], axis=0)
                v_bd = jnp.concatenate([jnp.where(lo, vt, zero), jnp.where(lo, zero, vt_r)], axis=0)
            else:
                k_bd = jnp.concatenate([jnp.where(lo, kt_r, zero), jnp.where(lo, zero, kt)], axis=0)
                v_bd = jnp.concatenate([jnp.where(lo, vt_r, zero), jnp.where(lo, zero, vt)], axis=0)
            for qp in range(2):
                tile = hkv * 2 + qp
                qs = slice(tile * LANES, (tile + 1) * LANES)
                s = _dot_nt(q_ref[:, qs], k_bd)
                p0 = softmax(s[:, :2 * w], sink_ref[2 * tile])
                p1 = softmax(s[:, 2 * w:], sink_ref[2 * tile + 1])
                probs = jnp.concatenate([p0, p1], axis=1).astype(BF16)
                o_ref[:, qs] = _dot(probs, v_bd).astype(o_ref.dtype)


def _swa(qkv, sinks, seq):
    t = qkv.shape[0]
    w = WINDOW
    nb = seq // w
    q_blocks = Q_A // KV_A

    def prev(i):
        return jnp.maximum(i - 1, 0)

    return pl.pallas_call(
        functools.partial(_swa_kernel, blocks_per_seq=nb),
        out_shape=jax.ShapeDtypeStruct((t, Q_A), BF16),
        grid=(t // w,),
        in_specs=[
            pl.BlockSpec(memory_space=pltpu.SMEM),
            pl.BlockSpec((w, Q_A), lambda i: (i, 0)),
            pl.BlockSpec((w, KV_A), lambda i: (i, q_blocks)),
            pl.BlockSpec((w, KV_A), lambda i: (prev(i), q_blocks)),
            pl.BlockSpec((w, KV_A), lambda i: (i, q_blocks + 1)),
            pl.BlockSpec((w, KV_A), lambda i: (prev(i), q_blocks + 1)),
        ],
        out_specs=pl.BlockSpec((w, Q_A), lambda i: (i, 0)),
        compiler_params=_params("parallel"),
        name="swa",
    )(sinks, qkv, qkv, qkv, qkv, qkv)


def _gla_kernel(q_ref, k_ref, v_ref, og_ref, lr_ref, wup_ref, bgk_ref, gn_ref, o_ref, st_ref):
    @pl.when(pl.program_id(1) == 0)
    def _():
        st_ref[...] = jnp.zeros_like(st_ref)

    c = CHUNK
    z = _dot(lr_ref[...], wup_ref[...]) + bgk_ref[...]
    g = jax.nn.log_sigmoid(z) / GK_NORMALIZER
    r_i = lax.broadcasted_iota(jnp.int32, (c, c), 0)
    c_i = lax.broadcasted_iota(jnp.int32, (c, c), 1)
    causal = r_i >= c_i
    tri = causal.astype(BF16)
    g1 = g.astype(BF16)
    rem = g - g1.astype(F32)
    g2 = rem.astype(BF16)
    g3 = (rem - g2.astype(F32)).astype(BF16)
    b = _dot(tri, g1) + _dot(tri, g2) + _dot(tri, g3)
    b_mid = b[c // 2:c // 2 + 1, :]
    b_last = b[c - 1:c, :]
    q = q_ref[...].astype(F32) * (DK_B ** -0.5)
    k = k_ref[...].astype(F32)
    qs = q * jnp.exp(b - b_mid)
    ks = k * jnp.exp(b_mid - b)
    q_in = (qs * jnp.exp(b_mid)).astype(BF16)
    k_st = (ks * jnp.exp(b_last - b_mid)).astype(BF16)
    decay = jnp.exp(b_last)
    qs = qs.astype(BF16)
    ks = ks.astype(BF16)
    gn = gn_ref[...]
    for h in range(H_B):
        sk = slice(h * DK_B, (h + 1) * DK_B)
        sv = slice(h * DV_B, (h + 1) * DV_B)
        v = v_ref[:, sv]
        a = jnp.where(causal, _dot_nt(qs[:, sk], ks[:, sk]), 0.0)
        st = st_ref[h]
        o = _dot(a.astype(BF16), v) + _dot_nt(q_in[:, sk], st.astype(BF16))
        st_ref[h] = st * decay[:, sk] + _dot_tn(v, k_st[:, sk])
        ms = jnp.mean(o * o, axis=-1, keepdims=True)
        og = og_ref[:, sv].astype(F32)
        out = (o * lax.rsqrt(ms + EPS) * gn) * (og * jax.nn.sigmoid(og))
        o_ref[:, sv] = out.astype(o_ref.dtype)


def _gla(pb, lr, w_up_pad, b_gk, gla_norm, batch, seq):
    t = pb.shape[0]
    c = CHUNK
    nc = seq // c

    def rows(b, s):
        return b * nc + s

    return pl.pallas_call(
        _gla_kernel,
        out_shape=jax.ShapeDtypeStruct((t, V_B), BF16),
        grid=(batch, nc),
        in_specs=[
            pl.BlockSpec((c, QK_B), lambda b, s: (rows(b, s), 0)),
            pl.BlockSpec((c, QK_B), lambda b, s: (rows(b, s), 1)),
            pl.BlockSpec((c, V_B), lambda b, s: (rows(b, s), 1)),
            pl.BlockSpec((c, V_B), lambda b, s: (rows(b, s), 2)),
            pl.BlockSpec((c, GK_PAD), lambda b, s: (rows(b, s), 0)),
            pl.BlockSpec((GK_PAD, QK_B), lambda b, s: (0, 0)),
            pl.BlockSpec((1, QK_B), lambda b, s: (0, 0)),
            pl.BlockSpec((1, DV_B), lambda b, s: (0, 0)),
        ],
        out_specs=pl.BlockSpec((c, V_B), lambda b, s: (rows(b, s), 0)),
        scratch_shapes=[pltpu.VMEM((H_B, DV_B, DK_B), F32)],
        compiler_params=_params("parallel", "arbitrary"),
        name="gla",
    )(pb, pb, pb, pb, lr, w_up_pad, b_gk, gla_norm)


def _merge_kernel(oa_ref, ob_ref, wa_ref, wb_ref, ga_ref, gb_ref, o_ref):
    ta = _dot(oa_ref[...], wa_ref[...])
    tb = _dot(ob_ref[...], wb_ref[...])
    out = ga_ref[...].astype(F32) * ta + gb_ref[...].astype(F32) * tb
    o_ref[...] = out.astype(o_ref.dtype)


def _merge(o_a, o_b, w_a, w_b, gates):
    t = o_a.shape[0]
    d = w_a.shape[1]
    tm, tn = TM_MERGE, TN_MERGE
    nj = d // tn
    return pl.pallas_call(
        _merge_kernel,
        out_shape=jax.ShapeDtypeStruct((t, d), BF16),
        grid=(t // tm, nj),
        in_specs=[
            pl.BlockSpec((tm, Q_A), lambda i, j: (i, 0)),
            pl.BlockSpec((tm, V_B), lambda i, j: (i, 0)),
            pl.BlockSpec((Q_A, tn), lambda i, j: (0, j)),
            pl.BlockSpec((V_B, tn), lambda i, j: (0, j)),
            pl.BlockSpec((tm, tn), lambda i, j: (i, j)),
            pl.BlockSpec((tm, tn), lambda i, j: (i, j + nj)),
        ],
        out_specs=pl.BlockSpec((tm, tn), lambda i, j: (i, j)),
        compiler_params=_params("parallel", "parallel"),
        name="merge",
    )(o_a, o_b, w_a, w_b, gates, gates)


def _rms_residual(x, gate, y, nw):
    ms = jnp.mean(y * y, axis=-1, keepdims=True)
    return x + gate * (y * lax.rsqrt(ms + EPS) * nw)


def _out_proj_kernel(a_ref, w_ref, x_ref, g_ref, nw_ref, o_ref):
    y = _dot(a_ref[...], w_ref[...])
    o_ref[...] = _rms_residual(x_ref[...], g_ref[...], y, nw_ref[...])


def _out_proj(a, w, x2, gate, nw, seq):
    t, k = a.shape
    d = w.shape[1]
    tm = TM_OUT
    per_seq = seq // tm
    return pl.pallas_call(
        _out_proj_kernel,
        out_shape=jax.ShapeDtypeStruct((t, d), F32),
        grid=(t // tm,),
        in_specs=[
            pl.BlockSpec((tm, k), lambda i: (i, 0)),
            pl.BlockSpec((k, d), lambda i: (0, 0)),
            pl.BlockSpec((tm, d), lambda i: (i, 0)),
            pl.BlockSpec((None, 1, d), lambda i: (i // per_seq, 0, 0)),
            pl.BlockSpec((1, d), lambda i: (0, 0)),
        ],
        out_specs=pl.BlockSpec((tm, d), lambda i: (i, 0)),
        compiler_params=_params("parallel"),
        name="out_proj",
    )(a, w, x2, gate, nw)


def _ffn_up_kernel(h_ref, wg_ref, wv_ref, cwg_ref, cwv_ref, cbg_ref, cbv_ref, o_ref, cg_ref, cv_ref,
                   *, steps_per_seq):
    @pl.when(pl.program_id(1) % steps_per_seq == 0)
    def _():
        cg_ref[...] = jnp.zeros_like(cg_ref)
        cv_ref[...] = jnp.zeros_like(cv_ref)

    h = h_ref[...]
    tm = h.shape[0]

    def conv(u, cw_ref, cb_ref, carry_ref):
        row = lax.broadcasted_iota(jnp.int32, u.shape, 0)
        c6 = carry_ref[SUBLANES - 2:SUBLANES - 1, :]
        c7 = carry_ref[SUBLANES - 1:SUBLANES, :]
        u1 = jnp.where(row == 0, c7, pltpu.roll(u, 1, 0))
        u2 = jnp.where(row == 0, c6, jnp.where(row == 1, c7, pltpu.roll(u, 2, 0)))
        carry_ref[...] = u[tm - SUBLANES:, :]
        return cw_ref[0:1, :] * u2 + cw_ref[1:2, :] * u1 + cw_ref[2:3, :] * u + cb_ref[...]

    gate = conv(_dot(h, wg_ref[...]), cwg_ref, cbg_ref, cg_ref)
    val = conv(_dot(h, wv_ref[...]), cwv_ref, cbv_ref, cv_ref)
    o_ref[...] = (gate * jax.nn.sigmoid(gate) * val).astype(o_ref.dtype)


def _ffn_up(h, w_up, conv_w, conv_b, seq):
    t, d = h.shape
    tm, tn = TM_UP, TN_UP
    nj = D_FF // tn
    return pl.pallas_call(
        functools.partial(_ffn_up_kernel, steps_per_seq=seq // tm),
        out_shape=jax.ShapeDtypeStruct((t, D_FF), BF16),
        grid=(nj, t // tm),
        in_specs=[
            pl.BlockSpec((tm, d), lambda j, i: (i, 0)),
            pl.BlockSpec((d, tn), lambda j, i: (0, j)),
            pl.BlockSpec((d, tn), lambda j, i: (0, j + nj)),
            pl.BlockSpec((CONV_WIDTH, tn), lambda j, i: (0, j)),
            pl.BlockSpec((CONV_WIDTH, tn), lambda j, i: (0, j + nj)),
            pl.BlockSpec((1, tn), lambda j, i: (0, j)),
            pl.BlockSpec((1, tn), lambda j, i: (0, j + nj)),
        ],
        out_specs=pl.BlockSpec((tm, tn), lambda j, i: (i, j)),
        scratch_shapes=[pltpu.VMEM((SUBLANES, tn), F32), pltpu.VMEM((SUBLANES, tn), F32)],
        compiler_params=_params("parallel", "arbitrary"),
        name="ffn_up",
    )(h, w_up, w_up, conv_w, conv_w, conv_b, conv_b)


def _ffn_down_kernel(a_ref, w_ref, x_ref, g_ref, nw_ref, o_ref, acc_ref):
    kk = pl.program_id(1)

    @pl.when(kk == 0)
    def _():
        acc_ref[...] = jnp.zeros_like(acc_ref)

    acc_ref[...] += _dot(a_ref[...], w_ref[...])

    @pl.when(kk == pl.num_programs(1) - 1)
    def _():
        o_ref[...] = _rms_residual(x_ref[...], g_ref[...], acc_ref[...], nw_ref[...])


def _ffn_down(a, w, x2, gate, nw, seq):
    t, k = a.shape
    d = w.shape[1]
    tm, tk = TM_DOWN, TK_DOWN
    per_seq = seq // tm
    return pl.pallas_call(
        _ffn_down_kernel,
        out_shape=jax.ShapeDtypeStruct((t, d), F32),
        grid=(t // tm, k // tk),
        in_specs=[
            pl.BlockSpec((tm, tk), lambda i, kk: (i, kk)),
            pl.BlockSpec((tk, d), lambda i, kk: (kk, 0)),
            pl.BlockSpec((tm, d), lambda i, kk: (i, 0)),
            pl.BlockSpec((None, 1, d), lambda i, kk: (i // per_seq, 0, 0)),
            pl.BlockSpec((1, d), lambda i, kk: (0, 0)),
        ],
        out_specs=pl.BlockSpec((tm, d), lambda i, kk: (i, 0)),
        scratch_shapes=[pltpu.VMEM((tm, d), F32)],
        compiler_params=_params("parallel", "arbitrary"),
        name="ffn_down",
    )(a, w, x2, gate, nw)


def kernel(x, c, positions, w_mod, b_mod, mix_norm_pre, mix_norm_post, w_in, attn_sinks, w_gk_up, b_gk, gla_norm, w_branch_attn, w_branch_gla, w_out, ffn_norm_pre, ffn_norm_post, w_up, conv_w, conv_b, w_down):
    batch, seq, d = x.shape
    t = batch * seq
    depth = w_mod.shape[0]
    x2 = x.reshape(t, d)

    half = HEAD_DIM_A // 2
    inv_freq = ROPE_THETA ** (-jnp.arange(0, HEAD_DIM_A, 2, dtype=F32) / HEAD_DIM_A)
    invf_row = jnp.tile(inv_freq, LANES // half).reshape(1, LANES)
    sgn_row = jnp.tile(jnp.concatenate([-jnp.ones((half,), F32), jnp.ones((half,), F32)]),
                       LANES // HEAD_DIM_A).reshape(1, LANES)
    cos_t, sin_t = _rope_tables(positions.reshape(t, 1), invf_row, sgn_row)

    c_pad = jnp.zeros((SUBLANES, d), F32).at[:batch].set(c)

    o_qb = Q_A + 2 * KV_A
    o_lr = o_qb + 2 * QK_B + V_B
    o_og = o_lr + GK_RANK
    o_ga = o_og + V_B

    for l in range(depth):
        mod = _modulation(c_pad, w_mod[l], b_mod[l].reshape(1, -1))[:batch]
        sh1, sc1, g1, sh2, sc2, g2 = [m.reshape(batch, 1, d) for m in jnp.split(mod, 6, axis=-1)]

        w_in_l = w_in[l].astype(BF16)
        w_a = jnp.concatenate(
            [w_in_l[:, :o_qb], w_in_l[:, o_lr:o_og], jnp.zeros((d, GK_PAD - GK_RANK), BF16)], axis=1)
        w_b = jnp.concatenate([w_in_l[:, o_qb:o_lr], w_in_l[:, o_og:o_ga]], axis=1)
        w_g = w_in_l[:, o_ga:]
        w_gk_pad = jnp.zeros((GK_PAD, QK_B), BF16).at[:GK_RANK].set(w_gk_up[l].astype(BF16))

        h = _norm_mod(x2, mix_norm_pre[l].reshape(1, d), sc1, sh1, seq)
        qkv_a, gk_lr = _proj_a(h, w_a, cos_t, sin_t)
        pb = _matmul(h, w_b, act=None, name="proj_b")
        gates = _matmul(h, w_g, act="sigmoid", name="gates")
        o_a = _swa(qkv_a, attn_sinks[l], seq)
        o_b = _gla(pb, gk_lr, w_gk_pad, b_gk[l].reshape(1, -1), gla_norm[l].reshape(1, -1), batch, seq)
        merged = _merge(o_a, o_b, w_branch_attn[l].astype(BF16), w_branch_gla[l].astype(BF16), gates)
        x2 = _out_proj(merged, w_out[l].astype(BF16), x2, g1, mix_norm_post[l].reshape(1, d), seq)

        h = _norm_mod(x2, ffn_norm_pre[l].reshape(1, d), sc2, sh2, seq)
        act = _ffn_up(h, w_up[l].astype(BF16), conv_w[l], conv_b[l].reshape(1, -1), seq)
        x2 = _ffn_down(act, w_down[l].astype(BF16), x2, g2, ffn_norm_post[l].reshape(1, d), seq)

    return x2.reshape(batch, seq, d)
```

```python
import functools

import jax
import jax.numpy as jnp
from jax import lax
from jax.experimental import pallas as pl
from jax.experimental.pallas import tpu as pltpu

F32 = jnp.float32
BF16 = jnp.bfloat16

D_MODEL = 2048
HEAD_DIM_A = 64
HQ_A = 16
HKV_A = 4
WINDOW = 128
ROPE_THETA = 10000.0
H_B = 4
DK_B = 256
DV_B = 512
GK_RANK = 16
GK_NORMALIZER = 16.0
CHUNK = 64
D_FF = 5632
CONV_WIDTH = 3
EPS = 1e-6

Q_A = HQ_A * HEAD_DIM_A
KV_A = HKV_A * HEAD_DIM_A
QK_B = H_B * DK_B
V_B = H_B * DV_B
QKV_A = Q_A + 2 * KV_A
ROPE_COLS = Q_A + KV_A

LANES = 128
SUBLANES = 8
GK_PAD = LANES
VMEM_LIMIT = 48 * 1024 * 1024

TM_NORM = 512
TM_ROPE = 2048
TM_PROJ_A = 512
TM_MM, TN_MM = 1024, 1024
TM_MERGE, TN_MERGE = 1024, 512
TM_OUT = 512
TM_FFN, TN_FFN = 512, 512
ROWS_GLA = 256
ROWS_EPILOGUE = 256
TN_MOD = 1024


def _params(*sem, vmem=VMEM_LIMIT):
    return pltpu.CompilerParams(dimension_semantics=sem, vmem_limit_bytes=vmem)


def _dot(a, b):
    return jnp.dot(a, b, preferred_element_type=F32)


def _dot_nt(a, b):
    return lax.dot_general(a, b, (((1,), (1,)), ((), ())), preferred_element_type=F32)


def _dot_tn(a, b):
    return lax.dot_general(a, b, (((0,), (0,)), ((), ())), preferred_element_type=F32)


def _mod_kernel(c_ref, w_ref, b_ref, o_ref):
    c = c_ref[...]
    c_act = (c * jax.nn.sigmoid(c)).astype(BF16)
    o_ref[...] = _dot(c_act, w_ref[...].astype(BF16)) + b_ref[...]


def _modulation(c_pad, w_mod, b_mod):
    rows, d = c_pad.shape
    n = w_mod.shape[1]
    return pl.pallas_call(
        _mod_kernel,
        out_shape=jax.ShapeDtypeStruct((rows, n), F32),
        grid=(n // TN_MOD,),
        in_specs=[
            pl.BlockSpec((rows, d), lambda j: (0, 0)),
            pl.BlockSpec((d, TN_MOD), lambda j: (0, j)),
            pl.BlockSpec((1, TN_MOD), lambda j: (0, j)),
        ],
        out_specs=pl.BlockSpec((rows, TN_MOD), lambda j: (0, j)),
        compiler_params=_params("parallel"),
        name="mod",
    )(c_pad, w_mod, b_mod)


def _norm_mod_kernel(x_ref, nw_ref, sc_ref, sh_ref, o_ref):
    x = x_ref[...]
    ms = jnp.mean(x * x, axis=-1, keepdims=True)
    y = x * lax.rsqrt(ms + EPS) * nw_ref[...]
    o_ref[...] = (y * (1.0 + sc_ref[...]) + sh_ref[...]).astype(o_ref.dtype)


def _norm_mod(x2, nw, sc, sh, seq):
    t, d = x2.shape
    per_seq = seq // TM_NORM
    return pl.pallas_call(
        _norm_mod_kernel,
        out_shape=jax.ShapeDtypeStruct((t, d), BF16),
        grid=(t // TM_NORM,),
        in_specs=[
            pl.BlockSpec((TM_NORM, d), lambda i: (i, 0)),
            pl.BlockSpec((1, d), lambda i: (0, 0)),
            pl.BlockSpec((None, 1, d), lambda i: (i // per_seq, 0, 0)),
            pl.BlockSpec((None, 1, d), lambda i: (i // per_seq, 0, 0)),
        ],
        out_specs=pl.BlockSpec((TM_NORM, d), lambda i: (i, 0)),
        compiler_params=_params("parallel"),
        name="norm_mod",
    )(x2, nw, sc, sh)


def _rope_table_kernel(pos_ref, invf_ref, sgn_ref, cos_ref, sin_ref):
    ang = pos_ref[...].astype(F32) * invf_ref[...]
    cos_ref[...] = jnp.cos(ang)
    sin_ref[...] = jnp.sin(ang) * sgn_ref[...]


def _rope_tables(pos_col, invf_row, sgn_row):
    t = pos_col.shape[0]
    return pl.pallas_call(
        _rope_table_kernel,
        out_shape=(jax.ShapeDtypeStruct((t, LANES), F32), jax.ShapeDtypeStruct((t, LANES), F32)),
        grid=(t // TM_ROPE,),
        in_specs=[
            pl.BlockSpec((TM_ROPE, 1), lambda i: (i, 0)),
            pl.BlockSpec((1, LANES), lambda i: (0, 0)),
            pl.BlockSpec((1, LANES), lambda i: (0, 0)),
        ],
        out_specs=(
            pl.BlockSpec((TM_ROPE, LANES), lambda i: (i, 0)),
            pl.BlockSpec((TM_ROPE, LANES), lambda i: (i, 0)),
        ),
        compiler_params=_params("parallel"),
        name="rope_tables",
    )(pos_col, invf_row, sgn_row)


def _proj_a_kernel(h_ref, w_ref, cos_ref, sin_ref, qkv_ref, lr_ref):
    acc = _dot(h_ref[...], w_ref[...])
    cos = cos_ref[...]
    sin = sin_ref[...]
    lane = lax.broadcasted_iota(jnp.int32, cos.shape, 1)
    first_half = (lane % HEAD_DIM_A) < (HEAD_DIM_A // 2)
    half = HEAD_DIM_A // 2
    for c in range(ROPE_COLS // LANES):
        y = acc[:, c * LANES:(c + 1) * LANES]
        partner = jnp.where(first_half, pltpu.roll(y, LANES - half, 1), pltpu.roll(y, half, 1))
        r = y * cos + partner * sin
        if c < Q_A // LANES:
            r = r * (HEAD_DIM_A ** -0.5)
        qkv_ref[:, c * LANES:(c + 1) * LANES] = r.astype(qkv_ref.dtype)
    qkv_ref[:, ROPE_COLS:QKV_A] = acc[:, ROPE_COLS:QKV_A].astype(qkv_ref.dtype)
    lr_ref[...] = acc[:, QKV_A:QKV_A + GK_PAD].astype(lr_ref.dtype)


def _proj_a(h, w_a, cos_t, sin_t):
    t, d = h.shape
    n = w_a.shape[1]
    tm = TM_PROJ_A
    return pl.pallas_call(
        _proj_a_kernel,
        out_shape=(jax.ShapeDtypeStruct((t, QKV_A), BF16), jax.ShapeDtypeStruct((t, GK_PAD), BF16)),
        grid=(t // tm,),
        in_specs=[
            pl.BlockSpec((tm, d), lambda i: (i, 0)),
            pl.BlockSpec((d, n), lambda i: (0, 0)),
            pl.BlockSpec((tm, LANES), lambda i: (i, 0)),
            pl.BlockSpec((tm, LANES), lambda i: (i, 0)),
        ],
        out_specs=(
            pl.BlockSpec((tm, QKV_A), lambda i: (i, 0)),
            pl.BlockSpec((tm, GK_PAD), lambda i: (i, 0)),
        ),
        compiler_params=_params("parallel"),
        name="proj_a",
    )(h, w_a, cos_t, sin_t)


def _mm_kernel(a_ref, w_ref, o_ref, *, act):
    acc = _dot(a_ref[...], w_ref[...])
    if act == "sigmoid":
        acc = jax.nn.sigmoid(acc)
    o_ref[...] = acc.astype(o_ref.dtype)


def _matmul(a, w, *, act, name):
    m, k = a.shape
    n = w.shape[1]
    return pl.pallas_call(
        functools.partial(_mm_kernel, act=act),
        out_shape=jax.ShapeDtypeStruct((m, n), BF16),
        grid=(m // TM_MM, n // TN_MM),
        in_specs=[
            pl.BlockSpec((TM_MM, k), lambda i, j: (i, 0)),
            pl.BlockSpec((k, TN_MM), lambda i, j: (0, j)),
        ],
        out_specs=pl.BlockSpec((TM_MM, TN_MM), lambda i, j: (i, j)),
        compiler_params=_params("parallel", "parallel"),
        name=name,
    )(a, w)


def _swa_kernel(sink_ref, q_ref, kc_ref, kp_ref, vc_ref, vp_ref, o_ref, *, blocks_per_seq):
    w = WINDOW
    dh = HEAD_DIM_A
    slot = lax.broadcasted_iota(jnp.int32, (w, w), 0)
    qry = lax.broadcasted_iota(jnp.int32, (w, w), 1)
    from_prev = slot > qry
    prev_bias = jnp.where((pl.program_id(0) % blocks_per_seq) > 0, 0.0, -jnp.inf)
    lo = lax.broadcasted_iota(jnp.int32, (2 * w, LANES), 1) < dh
    zero_k = jnp.zeros((2 * w, LANES), BF16)
    zero_v = jnp.zeros((dh, 2 * w), BF16)

    v_bdts = []
    windows = []
    for pair in range(HKV_A // 2):
        ls = slice(pair * LANES, (pair + 1) * LANES)
        kt = jnp.concatenate([kp_ref[:, ls], kc_ref[:, ls]], axis=0)
        kt_r = pltpu.roll(kt.astype(F32), dh, 1).astype(BF16)
        vt_t = jnp.concatenate([vp_ref[:, ls], vc_ref[:, ls]], axis=0).astype(F32).T.astype(BF16)
        for e in range(2):
            hkv = 2 * pair + e
            k_lo, k_hi = (kt, kt_r) if e == 0 else (kt_r, kt)
            k_bd = jnp.concatenate([jnp.where(lo, k_lo, zero_k), jnp.where(lo, zero_k, k_hi)], axis=0)
            v_t = vt_t[e * dh:(e + 1) * dh, :]
            v_bdts.append(jnp.concatenate([jnp.concatenate([v_t, zero_v], axis=1),
                                           jnp.concatenate([zero_v, v_t], axis=1)], axis=0))
            q2 = jnp.concatenate([q_ref[:, (2 * hkv + t) * LANES:(2 * hkv + t + 1) * LANES] for t in range(2)],
                                 axis=0)
            s_t = _dot_nt(k_bd, q2)
            for t in range(2):
                for hh in range(2):
                    blk = s_t[2 * w * hh:2 * w * (hh + 1), t * w:(t + 1) * w]
                    windows.append(jnp.where(from_prev, blk[:w, :] + prev_bias, blk[w:, :]))
    s_all = jnp.concatenate(windows, axis=1)
    sink = jnp.concatenate([jnp.full((1, w), sink_ref[n], F32) for n in range(HQ_A)], axis=1)
    m = jnp.maximum(jnp.max(s_all, axis=0, keepdims=True), sink)
    p = jnp.exp(s_all - m)
    denom = jnp.sum(p, axis=0, keepdims=True) + jnp.exp(sink - m)
    p = p * (1.0 / denom)

    zero_p = jnp.zeros((w, w), F32)
    for hkv in range(HKV_A):
        cols = []
        for t in range(2):
            parts = []
            for hh in range(2):
                n = 4 * hkv + 2 * t + hh
                ph = p[:, n * w:(n + 1) * w]
                parts += [jnp.where(from_prev, ph, zero_p), jnp.where(from_prev, zero_p, ph)]
            cols.append(jnp.concatenate(parts, axis=0))
        probs_t = jnp.concatenate(cols, axis=1).astype(BF16)
        o_t = _dot(v_bdts[hkv], probs_t)
        for t in range(2):
            tile = 2 * hkv + t
            o_ref[:, tile * LANES:(tile + 1) * LANES] = o_t[:, t * w:(t + 1) * w].T.astype(o_ref.dtype)


def _swa(qkv, sinks, seq):
    t = qkv.shape[0]
    w = WINDOW
    nb = seq // w
    q_blocks = Q_A // KV_A

    def prev(i):
        return jnp.maximum(i - 1, 0)

    return pl.pallas_call(
        functools.partial(_swa_kernel, blocks_per_seq=nb),
        out_shape=jax.ShapeDtypeStruct((t, Q_A), BF16),
        grid=(t // w,),
        in_specs=[
            pl.BlockSpec(memory_space=pltpu.SMEM),
            pl.BlockSpec((w, Q_A), lambda i: (i, 0)),
            pl.BlockSpec((w, KV_A), lambda i: (i, q_blocks)),
            pl.BlockSpec((w, KV_A), lambda i: (prev(i), q_blocks)),
            pl.BlockSpec((w, KV_A), lambda i: (i, q_blocks + 1)),
            pl.BlockSpec((w, KV_A), lambda i: (prev(i), q_blocks + 1)),
        ],
        out_specs=pl.BlockSpec((w, Q_A), lambda i: (i, 0)),
        compiler_params=_params("parallel"),
        name="swa",
    )(sinks, qkv, qkv, qkv, qkv, qkv)


def _gla_kernel(q_ref, k_ref, v_ref, og_ref, lr_ref, wup_ref, bgk_ref, gn_ref, o_ref, st_ref):
    @pl.when(pl.program_id(1) == 0)
    def _():
        st_ref[...] = jnp.zeros_like(st_ref)

    c = CHUNK
    rows_blk = q_ref.shape[0]
    nch = rows_blk // c
    mid = c // 2
    z = _dot(lr_ref[...], wup_ref[...]) + bgk_ref[...]
    g = jax.nn.log_sigmoid(z) / GK_NORMALIZER
    g1 = g.astype(BF16)
    rem = g - g1.astype(F32)
    g2 = rem.astype(BF16)
    g3 = (rem - g2.astype(F32)).astype(BF16)

    def sums(mat):
        return _dot(mat, g1) + _dot(mat, g2) + _dot(mat, g3)

    shift = c.bit_length() - 1
    r_i = lax.broadcasted_iota(jnp.int32, (rows_blk, rows_blk), 0)
    c_i = lax.broadcasted_iota(jnp.int32, (rows_blk, rows_blk), 1)
    same = (r_i >> shift) == (c_i >> shift)
    r_in = r_i & (c - 1)
    c_in = c_i & (c - 1)
    plus = same & (c_in <= r_in) & (c_in > mid)
    minus = same & (c_in > r_in) & (c_in <= mid)
    d_mid = jnp.where(plus, 1.0, 0.0) - jnp.where(minus, 1.0, 0.0)
    b_rel = sums(d_mid.astype(BF16))
    sel_rows = 2 * SUBLANES
    s_r = lax.broadcasted_iota(jnp.int32, (sel_rows, rows_blk), 0)
    s_c = lax.broadcasted_iota(jnp.int32, (sel_rows, rows_blk), 1)
    s_chunk = s_c >> shift
    pick = ((s_r == s_chunk) & ((s_c & (c - 1)) <= mid)) | (s_r == s_chunk + nch)
    b_sel = sums(jnp.where(pick, 1.0, 0.0).astype(BF16))

    q = q_ref[...].astype(F32) * (DK_B ** -0.5)
    k = k_ref[...].astype(F32)
    qs = q * jnp.exp(b_rel)
    ks = k * jnp.exp(-b_rel)
    qs_b = qs.astype(BF16)
    ks_b = ks.astype(BF16)
    gn = gn_ref[...]
    cr_i = lax.broadcasted_iota(jnp.int32, (c, c), 0)
    cc_i = lax.broadcasted_iota(jnp.int32, (c, c), 1)
    causal = cr_i >= cc_i
    for ch in range(nch):
        rows = slice(ch * c, (ch + 1) * c)
        b_mid = b_sel[ch:ch + 1, :]
        b_last = b_sel[nch + ch:nch + ch + 1, :]
        q_in = (qs[rows, :] * jnp.exp(b_mid)).astype(BF16)
        k_st = (ks[rows, :] * jnp.exp(b_last - b_mid)).astype(BF16)
        decay = jnp.exp(b_last)
        for h in range(H_B):
            sk = slice(h * DK_B, (h + 1) * DK_B)
            sv = slice(h * DV_B, (h + 1) * DV_B)
            v = v_ref[rows, sv]
            a = jnp.where(causal, _dot_nt(qs_b[rows, sk], ks_b[rows, sk]), 0.0)
            st = st_ref[h]
            o = _dot(a.astype(BF16), v) + _dot_nt(q_in[:, sk], st.astype(BF16))
            st_ref[h] = st * decay[:, sk] + _dot_tn(v, k_st[:, sk])
            ms = jnp.mean(o * o, axis=-1, keepdims=True)
            og = og_ref[rows, sv].astype(F32)
            out = (o * lax.rsqrt(ms + EPS) * gn) * (og * jax.nn.sigmoid(og))
            o_ref[rows, sv] = out.astype(o_ref.dtype)


def _gla(pb, lr, w_up_pad, b_gk, gla_norm, batch, seq):
    t = pb.shape[0]
    c = ROWS_GLA
    nc = seq // c

    def rows(b, s):
        return b * nc + s

    return pl.pallas_call(
        _gla_kernel,
        out_shape=jax.ShapeDtypeStruct((t, V_B), BF16),
        grid=(batch, nc),
        in_specs=[
            pl.BlockSpec((c, QK_B), lambda b, s: (rows(b, s), 0)),
            pl.BlockSpec((c, QK_B), lambda b, s: (rows(b, s), 1)),
            pl.BlockSpec((c, V_B), lambda b, s: (rows(b, s), 1)),
            pl.BlockSpec((c, V_B), lambda b, s: (rows(b, s), 2)),
            pl.BlockSpec((c, GK_PAD), lambda b, s: (rows(b, s), 0)),
            pl.BlockSpec((GK_PAD, QK_B), lambda b, s: (0, 0)),
            pl.BlockSpec((1, QK_B), lambda b, s: (0, 0)),
            pl.BlockSpec((1, DV_B), lambda b, s: (0, 0)),
        ],
        out_specs=pl.BlockSpec((c, V_B), lambda b, s: (rows(b, s), 0)),
        scratch_shapes=[pltpu.VMEM((H_B, DV_B, DK_B), F32)],
        compiler_params=_params("parallel", "arbitrary"),
        name="gla",
    )(pb, pb, pb, pb, lr, w_up_pad, b_gk, gla_norm)


def _merge_kernel(oa_ref, ob_ref, wa_ref, wb_ref, ga_ref, gb_ref, o_ref):
    ta = _dot(oa_ref[...], wa_ref[...])
    tb = _dot(ob_ref[...], wb_ref[...])
    out = ga_ref[...].astype(F32) * ta + gb_ref[...].astype(F32) * tb
    o_ref[...] = out.astype(o_ref.dtype)


def _merge(o_a, o_b, w_a, w_b, gates):
    t = o_a.shape[0]
    d = w_a.shape[1]
    tm, tn = TM_MERGE, TN_MERGE
    nj = d // tn
    return pl.pallas_call(
        _merge_kernel,
        out_shape=jax.ShapeDtypeStruct((t, d), BF16),
        grid=(t // tm, nj),
        in_specs=[
            pl.BlockSpec((tm, Q_A), lambda i, j: (i, 0)),
            pl.BlockSpec((tm, V_B), lambda i, j: (i, 0)),
            pl.BlockSpec((Q_A, tn), lambda i, j: (0, j)),
            pl.BlockSpec((V_B, tn), lambda i, j: (0, j)),
            pl.BlockSpec((tm, tn), lambda i, j: (i, j)),
            pl.BlockSpec((tm, tn), lambda i, j: (i, j + nj)),
        ],
        out_specs=pl.BlockSpec((tm, tn), lambda i, j: (i, j)),
        compiler_params=_params("parallel", "parallel"),
        name="merge",
    )(o_a, o_b, w_a, w_b, gates, gates)


def _rms_residual(x, gate, y, nw):
    ms = jnp.mean(y * y, axis=-1, keepdims=True)
    return x + gate * (y * lax.rsqrt(ms + EPS) * nw)


def _out_proj_kernel(a_ref, w_ref, x_ref, g_ref, nw_ref, nw2_ref, sc2_ref, sh2_ref, o_ref, h_ref):
    y = _dot(a_ref[...], w_ref[...])
    x1 = _rms_residual(x_ref[...], g_ref[...], y, nw_ref[...])
    o_ref[...] = x1
    ms = jnp.mean(x1 * x1, axis=-1, keepdims=True)
    h2 = x1 * lax.rsqrt(ms + EPS) * nw2_ref[...]
    h_ref[...] = (h2 * (1.0 + sc2_ref[...]) + sh2_ref[...]).astype(h_ref.dtype)


def _out_proj(a, w, x2, gate, nw, nw2, sc2, sh2, seq):
    t, k = a.shape
    d = w.shape[1]
    tm = TM_OUT
    per_seq = seq // tm
    row = pl.BlockSpec((1, d), lambda i: (0, 0))
    per_batch = pl.BlockSpec((None, 1, d), lambda i: (i // per_seq, 0, 0))
    return pl.pallas_call(
        _out_proj_kernel,
        out_shape=(jax.ShapeDtypeStruct((t, d), F32), jax.ShapeDtypeStruct((t, d), BF16)),
        grid=(t // tm,),
        in_specs=[
            pl.BlockSpec((tm, k), lambda i: (i, 0)),
            pl.BlockSpec((k, d), lambda i: (0, 0)),
            pl.BlockSpec((tm, d), lambda i: (i, 0)),
            per_batch, row, row, per_batch, per_batch,
        ],
        out_specs=(pl.BlockSpec((tm, d), lambda i: (i, 0)), pl.BlockSpec((tm, d), lambda i: (i, 0))),
        compiler_params=_params("parallel"),
        name="out_proj",
    )(a, w, x2, gate, nw, nw2, sc2, sh2)


def _ffn_kernel(h_ref, wg_ref, wv_ref, cwg_ref, cwv_ref, cbg_ref, cbv_ref, wd_ref, x_ref, g_ref, nw_ref,
                o_ref, acc_ref, act0_ref, act1_ref, carry_ref, *, nj, tiles_per_seq):
    i = pl.program_id(0)
    j = pl.program_id(1)
    tm = h_ref.shape[0]
    tn = wg_ref.shape[1]
    seq_start = (i % tiles_per_seq) == 0

    def up(act_ref):
        h = h_ref[...]
        jc = jnp.minimum(j, nj - 1)
        tails = []

        def conv(u, cols, cw_ref, cb_ref):
            row = lax.broadcasted_iota(jnp.int32, u.shape, 0)
            c6 = jnp.where(seq_start, 0.0, carry_ref[jc, SUBLANES - 2:SUBLANES - 1, cols])
            c7 = jnp.where(seq_start, 0.0, carry_ref[jc, SUBLANES - 1:SUBLANES, cols])
            u1 = jnp.where(row == 0, c7, pltpu.roll(u, 1, 0))
            u2 = jnp.where(row == 0, c6, jnp.where(row == 1, c7, pltpu.roll(u, 2, 0)))
            tails.append((cols, u[tm - SUBLANES:, :]))
            return cw_ref[0:1, :] * u2 + cw_ref[1:2, :] * u1 + cw_ref[2:3, :] * u + cb_ref[...]

        gate = conv(_dot(h, wg_ref[...]), slice(0, tn), cwg_ref, cbg_ref)
        val = conv(_dot(h, wv_ref[...]), slice(tn, 2 * tn), cwv_ref, cbv_ref)
        act_ref[...] = (gate * jax.nn.sigmoid(gate) * val).astype(act_ref.dtype)
        return jc, tails

    def save_carry(jc, tails):
        for cols, tail in tails:
            carry_ref[jc, :, cols] = tail

    def down(act_ref):
        acc_ref[...] += _dot(act_ref[...], wd_ref[...])

    @pl.when(j == 0)
    def _():
        acc_ref[...] = jnp.zeros_like(acc_ref)
        save_carry(*up(act0_ref))

    @pl.when((j > 0) & (j < nj) & (j % 2 == 1))
    def _():
        jc, tails = up(act1_ref)
        down(act0_ref)
        save_carry(jc, tails)

    @pl.when((j > 0) & (j < nj) & (j % 2 == 0))
    def _():
        jc, tails = up(act0_ref)
        down(act1_ref)
        save_carry(jc, tails)

    @pl.when(j == nj)
    def _():
        last = act0_ref if (nj - 1) % 2 == 0 else act1_ref
        down(last)
        for r in range(0, tm, ROWS_EPILOGUE):
            rows = slice(r, r + ROWS_EPILOGUE)
            o_ref[rows, :] = _rms_residual(x_ref[rows, :], g_ref[...], acc_ref[rows, :], nw_ref[...])


def _ffn(h, w_up, conv_w, conv_b, w_down, x2, gate, nw, seq):
    t, d = h.shape
    tm, tn = TM_FFN, TN_FFN
    nj = D_FF // tn
    tiles_per_seq = seq // tm

    def up_tile(j):
        return jnp.minimum(j, nj - 1)

    def down_tile(j):
        return jnp.maximum(j - 1, 0)

    return pl.pallas_call(
        functools.partial(_ffn_kernel, nj=nj, tiles_per_seq=tiles_per_seq),
        out_shape=jax.ShapeDtypeStruct((t, d), F32),
        grid=(t // tm, nj + 1),
        in_specs=[
            pl.BlockSpec((tm, d), lambda i, j: (i, 0)),
            pl.BlockSpec((d, tn), lambda i, j: (0, up_tile(j))),
            pl.BlockSpec((d, tn), lambda i, j: (0, up_tile(j) + nj)),
            pl.BlockSpec((CONV_WIDTH, tn), lambda i, j: (0, up_tile(j))),
            pl.BlockSpec((CONV_WIDTH, tn), lambda i, j: (0, up_tile(j) + nj)),
            pl.BlockSpec((1, tn), lambda i, j: (0, up_tile(j))),
            pl.BlockSpec((1, tn), lambda i, j: (0, up_tile(j) + nj)),
            pl.BlockSpec((tn, d), lambda i, j: (down_tile(j), 0)),
            pl.BlockSpec((tm, d), lambda i, j: (i, 0)),
            pl.BlockSpec((None, 1, d), lambda i, j: (i // tiles_per_seq, 0, 0)),
            pl.BlockSpec((1, d), lambda i, j: (0, 0)),
        ],
        out_specs=pl.BlockSpec((tm, d), lambda i, j: (i, 0)),
        scratch_shapes=[
            pltpu.VMEM((tm, d), F32),
            pltpu.VMEM((tm, tn), BF16),
            pltpu.VMEM((tm, tn), BF16),
            pltpu.VMEM((nj, SUBLANES, 2 * tn), F32),
        ],
        compiler_params=_params("arbitrary", "arbitrary"),
        name="ffn",
    )(h, w_up, w_up, conv_w, conv_w, conv_b, conv_b, w_down, x2, gate, nw)


def kernel(x, c, positions, w_mod, b_mod, mix_norm_pre, mix_norm_post, w_in, attn_sinks, w_gk_up, b_gk, gla_norm, w_branch_attn, w_branch_gla, w_out, ffn_norm_pre, ffn_norm_post, w_up, conv_w, conv_b, w_down):
    batch, seq, d = x.shape
    t = batch * seq
    depth = w_mod.shape[0]
    x2 = x.reshape(t, d)

    half = HEAD_DIM_A // 2
    inv_freq = ROPE_THETA ** (-jnp.arange(0, HEAD_DIM_A, 2, dtype=F32) / HEAD_DIM_A)
    invf_row = jnp.tile(inv_freq, LANES // half).reshape(1, LANES)
    sgn_row = jnp.tile(jnp.concatenate([-jnp.ones((half,), F32), jnp.ones((half,), F32)]),
                       LANES // HEAD_DIM_A).reshape(1, LANES)
    cos_t, sin_t = _rope_tables(positions.reshape(t, 1), invf_row, sgn_row)

    c_pad = jnp.zeros((SUBLANES, d), F32).at[:batch].set(c)

    o_qb = Q_A + 2 * KV_A
    o_lr = o_qb + 2 * QK_B + V_B
    o_og = o_lr + GK_RANK
    o_ga = o_og + V_B

    for l in range(depth):
        mod = _modulation(c_pad, w_mod[l], b_mod[l].reshape(1, -1))[:batch]
        sh1, sc1, g1, sh2, sc2, g2 = [m.reshape(batch, 1, d) for m in jnp.split(mod, 6, axis=-1)]

        w_in_l = w_in[l]
        w_a = jnp.concatenate(
            [w_in_l[:, :o_qb], w_in_l[:, o_lr:o_og], jnp.zeros((d, GK_PAD - GK_RANK), F32)], axis=1).astype(BF16)
        w_b = jnp.concatenate([w_in_l[:, o_qb:o_lr], w_in_l[:, o_og:o_ga]], axis=1).astype(BF16)
        w_g = w_in_l[:, o_ga:].astype(BF16)
        w_gk_pad = jnp.zeros((GK_PAD, QK_B), BF16).at[:GK_RANK].set(w_gk_up[l].astype(BF16))

        h = _norm_mod(x2, mix_norm_pre[l].reshape(1, d), sc1, sh1, seq)
        qkv_a, gk_lr = _proj_a(h, w_a, cos_t, sin_t)
        pb = _matmul(h, w_b, act=None, name="proj_b")
        gates = _matmul(h, w_g, act="sigmoid", name="gates")
        o_a = _swa(qkv_a, attn_sinks[l], seq)
        o_b = _gla(pb, gk_lr, w_gk_pad, b_gk[l].reshape(1, -1), gla_norm[l].reshape(1, -1), batch, seq)
        merged = _merge(o_a, o_b, w_branch_attn[l].astype(BF16), w_branch_gla[l].astype(BF16), gates)
        x2, h = _out_proj(merged, w_out[l].astype(BF16), x2, g1, mix_norm_post[l].reshape(1, d),
                          ffn_norm_pre[l].reshape(1, d), sc2, sh2, seq)

        x2 = _ffn(h, w_up[l].astype(BF16), conv_w[l], conv_b[l].reshape(1, -1), w_down[l].astype(BF16),
                  x2, g2, ffn_norm_post[l].reshape(1, d), seq)

    return x2.reshape(batch, seq, d)
```

```python
import functools

import jax
import jax.numpy as jnp
from jax import lax
from jax.experimental import pallas as pl
from jax.experimental.pallas import tpu as pltpu

F32 = jnp.float32
BF16 = jnp.bfloat16

D_MODEL = 2048
HEAD_DIM_A = 64
HQ_A = 16
HKV_A = 4
WINDOW = 128
ROPE_THETA = 10000.0
H_B = 4
DK_B = 256
DV_B = 512
GK_RANK = 16
GK_NORMALIZER = 16.0
CHUNK = 64
D_FF = 5632
CONV_WIDTH = 3
EPS = 1e-6

Q_A = HQ_A * HEAD_DIM_A
KV_A = HKV_A * HEAD_DIM_A
QK_B = H_B * DK_B
V_B = H_B * DV_B
QKV_A = Q_A + 2 * KV_A
ROPE_COLS = Q_A + KV_A

LANES = 128
SUBLANES = 8
GK_PAD = LANES
VMEM_LIMIT = 48 * 1024 * 1024
VMEM_LIMIT_FFN = 56 * 1024 * 1024

TM_NORM = 512
TM_ROPE = 2048
TM_PROJ_A = 512
TM_MM, TN_MM = 2048, 512
TM_MERGE, TN_MERGE = 1024, 512
TM_OUT = 512
TM_FFN, TN_FFN = 1024, 512
ROWS_GLA = 256
ROWS_EPILOGUE = 256
TN_MOD = 1024


def _params(*sem, vmem=VMEM_LIMIT):
    return pltpu.CompilerParams(dimension_semantics=sem, vmem_limit_bytes=vmem)


def _dot(a, b):
    return jnp.dot(a, b, preferred_element_type=F32)


def _dot_nt(a, b):
    return lax.dot_general(a, b, (((1,), (1,)), ((), ())), preferred_element_type=F32)


def _dot_tn(a, b):
    return lax.dot_general(a, b, (((0,), (0,)), ((), ())), preferred_element_type=F32)


def _mod_kernel(c_ref, w_ref, b_ref, o_ref):
    c = c_ref[...]
    c_act = (c * jax.nn.sigmoid(c)).astype(BF16)
    o_ref[...] = _dot(c_act, w_ref[...].astype(BF16)) + b_ref[...]


def _modulation(c_pad, w_mod, b_mod):
    rows, d = c_pad.shape
    n = w_mod.shape[1]
    return pl.pallas_call(
        _mod_kernel,
        out_shape=jax.ShapeDtypeStruct((rows, n), F32),
        grid=(n // TN_MOD,),
        in_specs=[
            pl.BlockSpec((rows, d), lambda j: (0, 0)),
            pl.BlockSpec((d, TN_MOD), lambda j: (0, j)),
            pl.BlockSpec((1, TN_MOD), lambda j: (0, j)),
        ],
        out_specs=pl.BlockSpec((rows, TN_MOD), lambda j: (0, j)),
        compiler_params=_params("parallel"),
        name="mod",
    )(c_pad, w_mod, b_mod)


def _norm_mod_kernel(x_ref, nw_ref, sc_ref, sh_ref, o_ref):
    x = x_ref[...]
    ms = jnp.mean(x * x, axis=-1, keepdims=True)
    y = x * lax.rsqrt(ms + EPS) * nw_ref[...]
    o_ref[...] = (y * (1.0 + sc_ref[...]) + sh_ref[...]).astype(o_ref.dtype)


def _norm_mod(x2, nw, sc, sh, seq):
    t, d = x2.shape
    per_seq = seq // TM_NORM
    return pl.pallas_call(
        _norm_mod_kernel,
        out_shape=jax.ShapeDtypeStruct((t, d), BF16),
        grid=(t // TM_NORM,),
        in_specs=[
            pl.BlockSpec((TM_NORM, d), lambda i: (i, 0)),
            pl.BlockSpec((1, d), lambda i: (0, 0)),
            pl.BlockSpec((None, 1, d), lambda i: (i // per_seq, 0, 0)),
            pl.BlockSpec((None, 1, d), lambda i: (i // per_seq, 0, 0)),
        ],
        out_specs=pl.BlockSpec((TM_NORM, d), lambda i: (i, 0)),
        compiler_params=_params("parallel"),
        name="norm_mod",
    )(x2, nw, sc, sh)


def _rope_table_kernel(pos_ref, invf_ref, sgn_ref, cos_ref, sin_ref):
    ang = pos_ref[...].astype(F32) * invf_ref[...]
    cos_ref[...] = jnp.cos(ang)
    sin_ref[...] = jnp.sin(ang) * sgn_ref[...]


def _rope_tables(pos_col, invf_row, sgn_row):
    t = pos_col.shape[0]
    return pl.pallas_call(
        _rope_table_kernel,
        out_shape=(jax.ShapeDtypeStruct((t, LANES), F32), jax.ShapeDtypeStruct((t, LANES), F32)),
        grid=(t // TM_ROPE,),
        in_specs=[
            pl.BlockSpec((TM_ROPE, 1), lambda i: (i, 0)),
            pl.BlockSpec((1, LANES), lambda i: (0, 0)),
            pl.BlockSpec((1, LANES), lambda i: (0, 0)),
        ],
        out_specs=(
            pl.BlockSpec((TM_ROPE, LANES), lambda i: (i, 0)),
            pl.BlockSpec((TM_ROPE, LANES), lambda i: (i, 0)),
        ),
        compiler_params=_params("parallel"),
        name="rope_tables",
    )(pos_col, invf_row, sgn_row)


def _proj_a_kernel(h_ref, w_ref, wlr_ref, cos_ref, sin_ref, qkv_ref, lr_ref, wbf_ref):
    @pl.when(pl.program_id(0) == 0)
    def _():
        wbf_ref[:, :QKV_A] = w_ref[...].astype(BF16)
        wbf_ref[:, QKV_A:] = wlr_ref[...].astype(BF16)

    acc = _dot(h_ref[...], wbf_ref[...])
    cos = cos_ref[...]
    sin = sin_ref[...]
    lane = lax.broadcasted_iota(jnp.int32, cos.shape, 1)
    first_half = (lane % HEAD_DIM_A) < (HEAD_DIM_A // 2)
    half = HEAD_DIM_A // 2
    for c in range(ROPE_COLS // LANES):
        y = acc[:, c * LANES:(c + 1) * LANES]
        partner = jnp.where(first_half, pltpu.roll(y, LANES - half, 1), pltpu.roll(y, half, 1))
        r = y * cos + partner * sin
        if c < Q_A // LANES:
            r = r * (HEAD_DIM_A ** -0.5)
        qkv_ref[:, c * LANES:(c + 1) * LANES] = r.astype(qkv_ref.dtype)
    qkv_ref[:, ROPE_COLS:QKV_A] = acc[:, ROPE_COLS:QKV_A].astype(qkv_ref.dtype)
    lr_ref[...] = acc[:, QKV_A:QKV_A + GK_PAD].astype(lr_ref.dtype)


def _proj_a(h, w_in, cos_t, sin_t):
    t, d = h.shape
    tm = TM_PROJ_A
    lr_tile = (QKV_A + 2 * QK_B + V_B) // GK_PAD
    assert lr_tile * GK_PAD == QKV_A + 2 * QK_B + V_B
    once = pl.Buffered(1)
    return pl.pallas_call(
        _proj_a_kernel,
        out_shape=(jax.ShapeDtypeStruct((t, QKV_A), BF16), jax.ShapeDtypeStruct((t, GK_PAD), BF16)),
        grid=(t // tm,),
        in_specs=[
            pl.BlockSpec((tm, d), lambda i: (i, 0)),
            pl.BlockSpec((d, QKV_A), lambda i: (0, 0), pipeline_mode=once),
            pl.BlockSpec((d, GK_PAD), lambda i: (0, lr_tile), pipeline_mode=once),
            pl.BlockSpec((tm, LANES), lambda i: (i, 0)),
            pl.BlockSpec((tm, LANES), lambda i: (i, 0)),
        ],
        out_specs=(
            pl.BlockSpec((tm, QKV_A), lambda i: (i, 0)),
            pl.BlockSpec((tm, GK_PAD), lambda i: (i, 0)),
        ),
        scratch_shapes=[pltpu.VMEM((d, QKV_A + GK_PAD), BF16)],
        compiler_params=_params("arbitrary"),
        name="proj_a",
    )(h, w_in, w_in, cos_t, sin_t)


def _mm_kernel(a_ref, w_ref, o_ref, wbf_ref, *, act):
    @pl.when(pl.program_id(1) == 0)
    def _():
        wbf_ref[...] = w_ref[...].astype(BF16)

    acc = _dot(a_ref[...], wbf_ref[...])
    if act == "sigmoid":
        acc = jax.nn.sigmoid(acc)
    o_ref[...] = acc.astype(o_ref.dtype)


def _matmul(a, w, *, col0, n, act, name):
    m, k = a.shape
    tm, tn = TM_MM, TN_MM
    assert col0 % tn == 0 and n % tn == 0
    j0 = col0 // tn
    return pl.pallas_call(
        functools.partial(_mm_kernel, act=act),
        out_shape=jax.ShapeDtypeStruct((m, n), BF16),
        grid=(n // tn, m // tm),
        in_specs=[
            pl.BlockSpec((tm, k), lambda j, i: (i, 0)),
            pl.BlockSpec((k, tn), lambda j, i: (0, j0 + j)),
        ],
        out_specs=pl.BlockSpec((tm, tn), lambda j, i: (i, j)),
        scratch_shapes=[pltpu.VMEM((k, tn), BF16)],
        compiler_params=_params("parallel", "arbitrary"),
        name=name,
    )(a, w)


def _swa_kernel(sink_ref, q_ref, kc_ref, kp_ref, vc_ref, vp_ref, o_ref, *, blocks_per_seq):
    w = WINDOW
    dh = HEAD_DIM_A
    slot = lax.broadcasted_iota(jnp.int32, (w, w), 0)
    qry = lax.broadcasted_iota(jnp.int32, (w, w), 1)
    from_prev = slot > qry
    prev_bias = jnp.where((pl.program_id(0) % blocks_per_seq) > 0, 0.0, -jnp.inf)
    lo = lax.broadcasted_iota(jnp.int32, (2 * w, LANES), 1) < dh
    zero_k = jnp.zeros((2 * w, LANES), BF16)
    zero_v = jnp.zeros((dh, 2 * w), BF16)

    v_bdts = []
    windows = []
    for pair in range(HKV_A // 2):
        ls = slice(pair * LANES, (pair + 1) * LANES)
        kt = jnp.concatenate([kp_ref[:, ls], kc_ref[:, ls]], axis=0)
        kt_r = pltpu.roll(kt.astype(F32), dh, 1).astype(BF16)
        vt_t = jnp.concatenate([vp_ref[:, ls], vc_ref[:, ls]], axis=0).astype(F32).T.astype(BF16)
        for e in range(2):
            hkv = 2 * pair + e
            k_lo, k_hi = (kt, kt_r) if e == 0 else (kt_r, kt)
            k_bd = jnp.concatenate([jnp.where(lo, k_lo, zero_k), jnp.where(lo, zero_k, k_hi)], axis=0)
            v_t = vt_t[e * dh:(e + 1) * dh, :]
            v_bdts.append(jnp.concatenate([jnp.concatenate([v_t, zero_v], axis=1),
                                           jnp.concatenate([zero_v, v_t], axis=1)], axis=0))
            q2 = jnp.concatenate([q_ref[:, (2 * hkv + t) * LANES:(2 * hkv + t + 1) * LANES] for t in range(2)],
                                 axis=0)
            s_t = _dot_nt(k_bd, q2)
            for t in range(2):
                for hh in range(2):
                    blk = s_t[2 * w * hh:2 * w * (hh + 1), t * w:(t + 1) * w]
                    windows.append(jnp.where(from_prev, blk[:w, :] + prev_bias, blk[w:, :]))
    s_all = jnp.concatenate(windows, axis=1)
    sink = jnp.concatenate([jnp.full((1, w), sink_ref[n], F32) for n in range(HQ_A)], axis=1)
    m = jnp.maximum(jnp.max(s_all, axis=0, keepdims=True), sink)
    p = jnp.exp(s_all - m)
    denom = jnp.sum(p, axis=0, keepdims=True) + jnp.exp(sink - m)
    p = p * (1.0 / denom)

    zero_p = jnp.zeros((w, w), F32)
    for hkv in range(HKV_A):
        cols = []
        for t in range(2):
            parts = []
            for hh in range(2):
                n = 4 * hkv + 2 * t + hh
                ph = p[:, n * w:(n + 1) * w]
                parts += [jnp.where(from_prev, ph, zero_p), jnp.where(from_prev, zero_p, ph)]
            cols.append(jnp.concatenate(parts, axis=0))
        probs_t = jnp.concatenate(cols, axis=1).astype(BF16)
        o_t = _dot(v_bdts[hkv], probs_t)
        for t in range(2):
            tile = 2 * hkv + t
            o_ref[:, tile * LANES:(tile + 1) * LANES] = o_t[:, t * w:(t + 1) * w].T.astype(o_ref.dtype)


def _swa(qkv, sinks, seq):
    t = qkv.shape[0]
    w = WINDOW
    nb = seq // w
    q_blocks = Q_A // KV_A

    def prev(i):
        return jnp.maximum(i - 1, 0)

    return pl.pallas_call(
        functools.partial(_swa_kernel, blocks_per_seq=nb),
        out_shape=jax.ShapeDtypeStruct((t, Q_A), BF16),
        grid=(t // w,),
        in_specs=[
            pl.BlockSpec(memory_space=pltpu.SMEM),
            pl.BlockSpec((w, Q_A), lambda i: (i, 0)),
            pl.BlockSpec((w, KV_A), lambda i: (i, q_blocks)),
            pl.BlockSpec((w, KV_A), lambda i: (prev(i), q_blocks)),
            pl.BlockSpec((w, KV_A), lambda i: (i, q_blocks + 1)),
            pl.BlockSpec((w, KV_A), lambda i: (prev(i), q_blocks + 1)),
        ],
        out_specs=pl.BlockSpec((w, Q_A), lambda i: (i, 0)),
        compiler_params=_params("parallel"),
        name="swa",
    )(sinks, qkv, qkv, qkv, qkv, qkv)


def _gla_kernel(q_ref, k_ref, v_ref, og_ref, lr_ref, wup_ref, bgk_ref, gn_ref, o_ref, st_ref):
    @pl.when(pl.program_id(1) == 0)
    def _():
        st_ref[...] = jnp.zeros_like(st_ref)

    c = CHUNK
    rows_blk = q_ref.shape[0]
    nch = rows_blk // c
    mid = c // 2
    z = _dot(lr_ref[...], wup_ref[...]) + bgk_ref[...]
    g = jax.nn.log_sigmoid(z) / GK_NORMALIZER
    g1 = g.astype(BF16)
    rem = g - g1.astype(F32)
    g2 = rem.astype(BF16)
    g3 = (rem - g2.astype(F32)).astype(BF16)

    def sums(mat):
        return _dot(mat, g1) + _dot(mat, g2) + _dot(mat, g3)

    shift = c.bit_length() - 1
    r_i = lax.broadcasted_iota(jnp.int32, (rows_blk, rows_blk), 0)
    c_i = lax.broadcasted_iota(jnp.int32, (rows_blk, rows_blk), 1)
    same = (r_i >> shift) == (c_i >> shift)
    r_in = r_i & (c - 1)
    c_in = c_i & (c - 1)
    plus = same & (c_in <= r_in) & (c_in > mid)
    minus = same & (c_in > r_in) & (c_in <= mid)
    d_mid = jnp.where(plus, 1.0, 0.0) - jnp.where(minus, 1.0, 0.0)
    b_rel = sums(d_mid.astype(BF16))
    sel_rows = 2 * SUBLANES
    s_r = lax.broadcasted_iota(jnp.int32, (sel_rows, rows_blk), 0)
    s_c = lax.broadcasted_iota(jnp.int32, (sel_rows, rows_blk), 1)
    s_chunk = s_c >> shift
    pick = ((s_r == s_chunk) & ((s_c & (c - 1)) <= mid)) | (s_r == s_chunk + nch)
    b_sel = sums(jnp.where(pick, 1.0, 0.0).astype(BF16))

    q = q_ref[...].astype(F32) * (DK_B ** -0.5)
    k = k_ref[...].astype(F32)
    qs = q * jnp.exp(b_rel)
    ks = k * jnp.exp(-b_rel)
    qs_b = qs.astype(BF16)
    ks_b = ks.astype(BF16)
    gn = gn_ref[...]
    cr_i = lax.broadcasted_iota(jnp.int32, (c, c), 0)
    cc_i = lax.broadcasted_iota(jnp.int32, (c, c), 1)
    causal = cr_i >= cc_i
    for ch in range(nch):
        rows = slice(ch * c, (ch + 1) * c)
        b_mid = b_sel[ch:ch + 1, :]
        b_last = b_sel[nch + ch:nch + ch + 1, :]
        q_in = (qs[rows, :] * jnp.exp(b_mid)).astype(BF16)
        k_st = (ks[rows, :] * jnp.exp(b_last - b_mid)).astype(BF16)
        decay = jnp.exp(b_last)
        for h in range(H_B):
            sk = slice(h * DK_B, (h + 1) * DK_B)
            sv = slice(h * DV_B, (h + 1) * DV_B)
            v = v_ref[rows, sv]
            a = jnp.where(causal, _dot_nt(qs_b[rows, sk], ks_b[rows, sk]), 0.0)
            st = st_ref[h]
            o = _dot(a.astype(BF16), v) + _dot_nt(q_in[:, sk], st.astype(BF16))
            st_ref[h] = st * decay[:, sk] + _dot_tn(v, k_st[:, sk])
            ms = jnp.mean(o * o, axis=-1, keepdims=True)
            og = og_ref[rows, sv].astype(F32)
            out = (o * lax.rsqrt(ms + EPS) * gn) * (og * jax.nn.sigmoid(og))
            o_ref[rows, sv] = out.astype(o_ref.dtype)


def _gla(pb, og, lr, w_up_pad, b_gk, gla_norm, batch, seq):
    t = pb.shape[0]
    c = ROWS_GLA
    nc = seq // c

    def rows(b, s):
        return b * nc + s

    return pl.pallas_call(
        _gla_kernel,
        out_shape=jax.ShapeDtypeStruct((t, V_B), BF16),
        grid=(batch, nc),
        in_specs=[
            pl.BlockSpec((c, QK_B), lambda b, s: (rows(b, s), 0)),
            pl.BlockSpec((c, QK_B), lambda b, s: (rows(b, s), 1)),
            pl.BlockSpec((c, V_B), lambda b, s: (rows(b, s), 1)),
            pl.BlockSpec((c, V_B), lambda b, s: (rows(b, s), 0)),
            pl.BlockSpec((c, GK_PAD), lambda b, s: (rows(b, s), 0)),
            pl.BlockSpec((GK_PAD, QK_B), lambda b, s: (0, 0)),
            pl.BlockSpec((1, QK_B), lambda b, s: (0, 0)),
            pl.BlockSpec((1, DV_B), lambda b, s: (0, 0)),
        ],
        out_specs=pl.BlockSpec((c, V_B), lambda b, s: (rows(b, s), 0)),
        scratch_shapes=[pltpu.VMEM((H_B, DV_B, DK_B), F32)],
        compiler_params=_params("parallel", "arbitrary"),
        name="gla",
    )(pb, pb, pb, og, lr, w_up_pad, b_gk, gla_norm)


def _merge_kernel(oa_ref, ob_ref, wa_ref, wb_ref, ga_ref, gb_ref, o_ref):
    ta = _dot(oa_ref[...], wa_ref[...])
    tb = _dot(ob_ref[...], wb_ref[...])
    out = ga_ref[...].astype(F32) * ta + gb_ref[...].astype(F32) * tb
    o_ref[...] = out.astype(o_ref.dtype)


def _merge(o_a, o_b, w_a, w_b, gates):
    t = o_a.shape[0]
    d = w_a.shape[1]
    tm, tn = TM_MERGE, TN_MERGE
    nj = d // tn
    return pl.pallas_call(
        _merge_kernel,
        out_shape=jax.ShapeDtypeStruct((t, d), BF16),
        grid=(t // tm, nj),
        in_specs=[
            pl.BlockSpec((tm, Q_A), lambda i, j: (i, 0)),
            pl.BlockSpec((tm, V_B), lambda i, j: (i, 0)),
            pl.BlockSpec((Q_A, tn), lambda i, j: (0, j)),
            pl.BlockSpec((V_B, tn), lambda i, j: (0, j)),
            pl.BlockSpec((tm, tn), lambda i, j: (i, j)),
            pl.BlockSpec((tm, tn), lambda i, j: (i, j + nj)),
        ],
        out_specs=pl.BlockSpec((tm, tn), lambda i, j: (i, j)),
        compiler_params=_params("parallel", "parallel"),
        name="merge",
    )(o_a, o_b, w_a, w_b, gates, gates)


def _rms_residual(x, gate, y, nw):
    ms = jnp.mean(y * y, axis=-1, keepdims=True)
    return x + gate * (y * lax.rsqrt(ms + EPS) * nw)


def _out_proj_kernel(a_ref, w_ref, x_ref, g_ref, nw_ref, nw2_ref, sc2_ref, sh2_ref, o_ref, h_ref):
    y = _dot(a_ref[...], w_ref[...])
    x1 = _rms_residual(x_ref[...], g_ref[...], y, nw_ref[...])
    o_ref[...] = x1
    ms = jnp.mean(x1 * x1, axis=-1, keepdims=True)
    h2 = x1 * lax.rsqrt(ms + EPS) * nw2_ref[...]
    h_ref[...] = (h2 * (1.0 + sc2_ref[...]) + sh2_ref[...]).astype(h_ref.dtype)


def _out_proj(a, w, x2, gate, nw, nw2, sc2, sh2, seq):
    t, k = a.shape
    d = w.shape[1]
    tm = TM_OUT
    per_seq = seq // tm
    row = pl.BlockSpec((1, d), lambda i: (0, 0))
    per_batch = pl.BlockSpec((None, 1, d), lambda i: (i // per_seq, 0, 0))
    return pl.pallas_call(
        _out_proj_kernel,
        out_shape=(jax.ShapeDtypeStruct((t, d), F32), jax.ShapeDtypeStruct((t, d), BF16)),
        grid=(t // tm,),
        in_specs=[
            pl.BlockSpec((tm, k), lambda i: (i, 0)),
            pl.BlockSpec((k, d), lambda i: (0, 0)),
            pl.BlockSpec((tm, d), lambda i: (i, 0)),
            per_batch, row, row, per_batch, per_batch,
        ],
        out_specs=(pl.BlockSpec((tm, d), lambda i: (i, 0)), pl.BlockSpec((tm, d), lambda i: (i, 0))),
        compiler_params=_params("parallel"),
        name="out_proj",
    )(a, w, x2, gate, nw, nw2, sc2, sh2)


def _ffn_kernel(h_ref, wg_ref, wv_ref, cwg_ref, cwv_ref, cbg_ref, cbv_ref, wd_ref, x_hbm, g_ref, nw_ref,
                o_ref, x_ref, x_sem, act0_ref, act1_ref, carry_ref, *, nj, tiles_per_seq):
    i = pl.program_id(0)
    j = pl.program_id(1)
    tm = h_ref.shape[0]
    tn = wg_ref.shape[1]
    seq_start = (i % tiles_per_seq) == 0

    def x_copy():
        return pltpu.make_async_copy(x_hbm.at[pl.ds(pl.multiple_of(i * tm, tm), tm), :], x_ref, x_sem)

    def up(act_ref):
        h = h_ref[...]
        jc = jnp.minimum(j, nj - 1)
        tails = []

        def conv(u, cols, cw_ref, cb_ref):
            row = lax.broadcasted_iota(jnp.int32, u.shape, 0)
            c6 = jnp.where(seq_start, 0.0, carry_ref[jc, SUBLANES - 2:SUBLANES - 1, cols])
            c7 = jnp.where(seq_start, 0.0, carry_ref[jc, SUBLANES - 1:SUBLANES, cols])
            u1 = jnp.where(row == 0, c7, pltpu.roll(u, 1, 0))
            u2 = jnp.where(row == 0, c6, jnp.where(row == 1, c7, pltpu.roll(u, 2, 0)))
            tails.append((cols, u[tm - SUBLANES:, :]))
            return cw_ref[0:1, :] * u2 + cw_ref[1:2, :] * u1 + cw_ref[2:3, :] * u + cb_ref[...]

        gate = conv(_dot(h, wg_ref[...]), slice(0, tn), cwg_ref, cbg_ref)
        val = conv(_dot(h, wv_ref[...]), slice(tn, 2 * tn), cwv_ref, cbv_ref)
        act_ref[...] = (gate * jax.nn.sigmoid(gate) * val).astype(act_ref.dtype)
        return jc, tails

    def save_carry(jc, tails):
        for cols, tail in tails:
            carry_ref[jc, :, cols] = tail

    def down(act_ref):
        o_ref[...] += _dot(act_ref[...], wd_ref[...])

    @pl.when(j == 0)
    def _():
        x_copy().start()
        o_ref[...] = jnp.zeros_like(o_ref)
        save_carry(*up(act0_ref))

    @pl.when((j > 0) & (j < nj) & (j % 2 == 1))
    def _():
        jc, tails = up(act1_ref)
        down(act0_ref)
        save_carry(jc, tails)

    @pl.when((j > 0) & (j < nj) & (j % 2 == 0))
    def _():
        jc, tails = up(act0_ref)
        down(act1_ref)
        save_carry(jc, tails)

    @pl.when(j == nj)
    def _():
        last = act0_ref if (nj - 1) % 2 == 0 else act1_ref
        down(last)
        x_copy().wait()
        for r in range(0, tm, ROWS_EPILOGUE):
            rows = slice(r, r + ROWS_EPILOGUE)
            o_ref[rows, :] = _rms_residual(x_ref[rows, :], g_ref[...], o_ref[rows, :], nw_ref[...])


def _ffn(h, w_up, conv_w, conv_b, w_down, x2, gate, nw, seq):
    t, d = h.shape
    tm, tn = TM_FFN, TN_FFN
    nj = D_FF // tn
    tiles_per_seq = seq // tm

    def up_tile(j):
        return jnp.minimum(j, nj - 1)

    def down_tile(j):
        return jnp.maximum(j - 1, 0)

    return pl.pallas_call(
        functools.partial(_ffn_kernel, nj=nj, tiles_per_seq=tiles_per_seq),
        out_shape=jax.ShapeDtypeStruct((t, d), F32),
        grid=(t // tm, nj + 1),
        in_specs=[
            pl.BlockSpec((tm, d), lambda i, j: (i, 0)),
            pl.BlockSpec((None, d, tn), lambda i, j: (up_tile(j), 0, 0)),
            pl.BlockSpec((None, d, tn), lambda i, j: (up_tile(j) + nj, 0, 0)),
            pl.BlockSpec((CONV_WIDTH, tn), lambda i, j: (0, up_tile(j))),
            pl.BlockSpec((CONV_WIDTH, tn), lambda i, j: (0, up_tile(j) + nj)),
            pl.BlockSpec((1, tn), lambda i, j: (0, up_tile(j))),
            pl.BlockSpec((1, tn), lambda i, j: (0, up_tile(j) + nj)),
            pl.BlockSpec((tn, d), lambda i, j: (down_tile(j), 0)),
            pl.BlockSpec(memory_space=pl.ANY),
            pl.BlockSpec((None, 1, d), lambda i, j: (i // tiles_per_seq, 0, 0)),
            pl.BlockSpec((1, d), lambda i, j: (0, 0)),
        ],
        out_specs=pl.BlockSpec((tm, d), lambda i, j: (i, 0)),
        scratch_shapes=[
            pltpu.VMEM((tm, d), F32),
            pltpu.SemaphoreType.DMA,
            pltpu.VMEM((tm, tn), BF16),
            pltpu.VMEM((tm, tn), BF16),
            pltpu.VMEM((nj, SUBLANES, 2 * tn), F32),
        ],
        compiler_params=_params("arbitrary", "arbitrary", vmem=VMEM_LIMIT_FFN),
        name="ffn",
    )(h, w_up, w_up, conv_w, conv_w, conv_b, conv_b, w_down, x2, gate, nw)


def kernel(x, c, positions, w_mod, b_mod, mix_norm_pre, mix_norm_post, w_in, attn_sinks, w_gk_up, b_gk, gla_norm, w_branch_attn, w_branch_gla, w_out, ffn_norm_pre, ffn_norm_post, w_up, conv_w, conv_b, w_down):
    batch, seq, d = x.shape
    t = batch * seq
    depth = w_mod.shape[0]
    x2 = x.reshape(t, d)

    half = HEAD_DIM_A // 2
    inv_freq = ROPE_THETA ** (-jnp.arange(0, HEAD_DIM_A, 2, dtype=F32) / HEAD_DIM_A)
    invf_row = jnp.tile(inv_freq, LANES // half).reshape(1, LANES)
    sgn_row = jnp.tile(jnp.concatenate([-jnp.ones((half,), F32), jnp.ones((half,), F32)]),
                       LANES // HEAD_DIM_A).reshape(1, LANES)
    cos_t, sin_t = _rope_tables(positions.reshape(t, 1), invf_row, sgn_row)

    c_pad = jnp.zeros((SUBLANES, d), F32).at[:batch].set(c)

    o_qb = Q_A + 2 * KV_A
    o_lr = o_qb + 2 * QK_B + V_B
    o_og = o_lr + GK_RANK

    for l in range(depth):
        mod = _modulation(c_pad, w_mod[l], b_mod[l].reshape(1, -1))[:batch]
        sh1, sc1, g1, sh2, sc2, g2 = [m.reshape(batch, 1, d) for m in jnp.split(mod, 6, axis=-1)]

        w_in_l = w_in[l]
        w_tail = w_in_l[:, o_og:]
        w_gk_pad = jnp.zeros((GK_PAD, QK_B), BF16).at[:GK_RANK].set(w_gk_up[l].astype(BF16))
        nj = D_FF // TN_FFN
        w_up_tiles = w_up[l].astype(BF16).reshape(d, 2 * nj, TN_FFN).transpose(1, 0, 2)

        h = _norm_mod(x2, mix_norm_pre[l].reshape(1, d), sc1, sh1, seq)
        qkv_a, gk_lr = _proj_a(h, w_in_l, cos_t, sin_t)
        pb = _matmul(h, w_in_l, col0=o_qb, n=o_lr - o_qb, act=None, name="proj_b")
        og = _matmul(h, w_tail, col0=0, n=V_B, act=None, name="proj_og")
        gates = _matmul(h, w_tail, col0=V_B, n=2 * d, act="sigmoid", name="gates")
        o_a = _swa(qkv_a, attn_sinks[l], seq)
        o_b = _gla(pb, og, gk_lr, w_gk_pad, b_gk[l].reshape(1, -1), gla_norm[l].reshape(1, -1), batch, seq)
        merged = _merge(o_a, o_b, w_branch_attn[l].astype(BF16), w_branch_gla[l].astype(BF16), gates)
        x2, h = _out_proj(merged, w_out[l].astype(BF16), x2, g1, mix_norm_post[l].reshape(1, d),
                          ffn_norm_pre[l].reshape(1, d), sc2, sh2, seq)

        x2 = _ffn(h, w_up_tiles, conv_w[l], conv_b[l].reshape(1, -1), w_down[l].astype(BF16),
                  x2, g2, ffn_norm_post[l].reshape(1, d), seq)

    return x2.reshape(batch, seq, d)
```

```python
import functools

import jax
import jax.numpy as jnp
from jax import lax
from jax.experimental import pallas as pl
from jax.experimental.pallas import tpu as pltpu

F32 = jnp.float32
BF16 = jnp.bfloat16

D_MODEL = 2048
HEAD_DIM_A = 64
HQ_A = 16
HKV_A = 4
WINDOW = 128
ROPE_THETA = 10000.0
H_B = 4
DK_B = 256
DV_B = 512
GK_RANK = 16
GK_NORMALIZER = 16.0
CHUNK = 64
D_FF = 5632
CONV_WIDTH = 3
EPS = 1e-6

Q_A = HQ_A * HEAD_DIM_A
KV_A = HKV_A * HEAD_DIM_A
QK_B = H_B * DK_B
V_B = H_B * DV_B
QKV_A = Q_A + 2 * KV_A
ROPE_COLS = Q_A + KV_A

LANES = 128
SUBLANES = 8
GK_PAD = LANES
VMEM_LIMIT = 48 * 1024 * 1024
VMEM_LIMIT_FFN = 56 * 1024 * 1024

TM_NORM = 512
TM_ROPE = 2048
TM_PROJ_A = 512
TM_MM, TN_MM = 1024, 1024
LR_ROWS = 16
TM_MERGE, TN_MERGE = 1024, 512
TM_OUT = 512
TM_FFN, TN_FFN = 1024, 512
ROWS_GLA = 256
ROWS_EPILOGUE = 256
TN_MOD = 1024


def _params(*sem, vmem=VMEM_LIMIT):
    return pltpu.CompilerParams(dimension_semantics=sem, vmem_limit_bytes=vmem)


def _dot(a, b):
    return jnp.dot(a, b, preferred_element_type=F32)


def _dot_nt(a, b):
    return lax.dot_general(a, b, (((1,), (1,)), ((), ())), preferred_element_type=F32)


def _dot_tn(a, b):
    return lax.dot_general(a, b, (((0,), (0,)), ((), ())), preferred_element_type=F32)


def _mod_kernel(c_ref, w_ref, b_ref, o_ref):
    c = c_ref[...]
    c_act = (c * jax.nn.sigmoid(c)).astype(BF16)
    o_ref[...] = _dot(c_act, w_ref[...].astype(BF16)) + b_ref[...]


def _modulation(c_pad, w_mod, b_mod):
    rows, d = c_pad.shape
    n = w_mod.shape[1]
    return pl.pallas_call(
        _mod_kernel,
        out_shape=jax.ShapeDtypeStruct((rows, n), F32),
        grid=(n // TN_MOD,),
        in_specs=[
            pl.BlockSpec((rows, d), lambda j: (0, 0)),
            pl.BlockSpec((d, TN_MOD), lambda j: (0, j)),
            pl.BlockSpec((1, TN_MOD), lambda j: (0, j)),
        ],
        out_specs=pl.BlockSpec((rows, TN_MOD), lambda j: (0, j)),
        compiler_params=_params("parallel"),
        name="mod",
    )(c_pad, w_mod, b_mod)


def _norm_mod_kernel(x_ref, nw_ref, sc_ref, sh_ref, o_ref):
    x = x_ref[...]
    ms = jnp.mean(x * x, axis=-1, keepdims=True)
    y = x * lax.rsqrt(ms + EPS) * nw_ref[...]
    o_ref[...] = (y * (1.0 + sc_ref[...]) + sh_ref[...]).astype(o_ref.dtype)


def _norm_mod(x2, nw, sc, sh, seq):
    t, d = x2.shape
    per_seq = seq // TM_NORM
    return pl.pallas_call(
        _norm_mod_kernel,
        out_shape=jax.ShapeDtypeStruct((t, d), BF16),
        grid=(t // TM_NORM,),
        in_specs=[
            pl.BlockSpec((TM_NORM, d), lambda i: (i, 0)),
            pl.BlockSpec((1, d), lambda i: (0, 0)),
            pl.BlockSpec((None, 1, d), lambda i: (i // per_seq, 0, 0)),
            pl.BlockSpec((None, 1, d), lambda i: (i // per_seq, 0, 0)),
        ],
        out_specs=pl.BlockSpec((TM_NORM, d), lambda i: (i, 0)),
        compiler_params=_params("parallel"),
        name="norm_mod",
    )(x2, nw, sc, sh)


def _rope_table_kernel(pos_ref, invf_ref, sgn_ref, cos_ref, sin_ref):
    ang = pos_ref[...].astype(F32) * invf_ref[...]
    cos_ref[...] = jnp.cos(ang)
    sin_ref[...] = jnp.sin(ang) * sgn_ref[...]


def _rope_tables(pos_col, invf_row, sgn_row):
    t = pos_col.shape[0]
    return pl.pallas_call(
        _rope_table_kernel,
        out_shape=(jax.ShapeDtypeStruct((t, LANES), F32), jax.ShapeDtypeStruct((t, LANES), F32)),
        grid=(t // TM_ROPE,),
        in_specs=[
            pl.BlockSpec((TM_ROPE, 1), lambda i: (i, 0)),
            pl.BlockSpec((1, LANES), lambda i: (0, 0)),
            pl.BlockSpec((1, LANES), lambda i: (0, 0)),
        ],
        out_specs=(
            pl.BlockSpec((TM_ROPE, LANES), lambda i: (i, 0)),
            pl.BlockSpec((TM_ROPE, LANES), lambda i: (i, 0)),
        ),
        compiler_params=_params("parallel"),
        name="rope_tables",
    )(pos_col, invf_row, sgn_row)


def _proj_a_kernel(h_ref, w_ref, wlr_ref, cos_ref, sin_ref, qkv_ref, lr_ref, wbf_ref):
    @pl.when(pl.program_id(0) == 0)
    def _():
        wbf_ref[:QKV_A, :] = w_ref[...].astype(BF16)
        wbf_ref[QKV_A:QKV_A + LR_ROWS, :] = wlr_ref[...].astype(BF16)
        wbf_ref[QKV_A + LR_ROWS:, :] = jnp.zeros((GK_PAD - LR_ROWS, wbf_ref.shape[1]), BF16)

    acc = _dot_nt(h_ref[...], wbf_ref[...])
    cos = cos_ref[...]
    sin = sin_ref[...]
    lane = lax.broadcasted_iota(jnp.int32, cos.shape, 1)
    first_half = (lane % HEAD_DIM_A) < (HEAD_DIM_A // 2)
    half = HEAD_DIM_A // 2
    for c in range(ROPE_COLS // LANES):
        y = acc[:, c * LANES:(c + 1) * LANES]
        partner = jnp.where(first_half, pltpu.roll(y, LANES - half, 1), pltpu.roll(y, half, 1))
        r = y * cos + partner * sin
        if c < Q_A // LANES:
            r = r * (HEAD_DIM_A ** -0.5)
        qkv_ref[:, c * LANES:(c + 1) * LANES] = r.astype(qkv_ref.dtype)
    qkv_ref[:, ROPE_COLS:QKV_A] = acc[:, ROPE_COLS:QKV_A].astype(qkv_ref.dtype)
    lr_ref[...] = acc[:, QKV_A:QKV_A + GK_PAD].astype(lr_ref.dtype)


def _proj_a(h, w_in_t, lr_row0, cos_t, sin_t):
    t, d = h.shape
    tm = TM_PROJ_A
    once = pl.Buffered(1)
    return pl.pallas_call(
        _proj_a_kernel,
        out_shape=(jax.ShapeDtypeStruct((t, QKV_A), BF16), jax.ShapeDtypeStruct((t, GK_PAD), BF16)),
        grid=(t // tm,),
        in_specs=[
            pl.BlockSpec((tm, d), lambda i: (i, 0)),
            pl.BlockSpec((QKV_A, d), lambda i: (0, 0), pipeline_mode=once),
            pl.BlockSpec((pl.Element(LR_ROWS), pl.Element(d)), lambda i: (lr_row0, 0), pipeline_mode=once),
            pl.BlockSpec((tm, LANES), lambda i: (i, 0)),
            pl.BlockSpec((tm, LANES), lambda i: (i, 0)),
        ],
        out_specs=(
            pl.BlockSpec((tm, QKV_A), lambda i: (i, 0)),
            pl.BlockSpec((tm, GK_PAD), lambda i: (i, 0)),
        ),
        scratch_shapes=[pltpu.VMEM((QKV_A + GK_PAD, d), BF16)],
        compiler_params=_params("arbitrary"),
        name="proj_a",
    )(h, w_in_t, w_in_t, cos_t, sin_t)


def _mm_kernel(a_ref, wt_ref, o_ref, wbf_ref, *, act):
    @pl.when(pl.program_id(1) == 0)
    def _():
        wbf_ref[...] = wt_ref[...].astype(BF16)

    acc = _dot_nt(a_ref[...], wbf_ref[...])
    if act == "sigmoid":
        acc = jax.nn.sigmoid(acc)
    o_ref[...] = acc.astype(o_ref.dtype)


def _matmul(a, w_t, *, row0, n, act, name):
    m, k = a.shape
    tm, tn = TM_MM, TN_MM
    assert row0 % SUBLANES == 0 and n % tn == 0
    return pl.pallas_call(
        functools.partial(_mm_kernel, act=act),
        out_shape=jax.ShapeDtypeStruct((m, n), BF16),
        grid=(n // tn, m // tm),
        in_specs=[
            pl.BlockSpec((tm, k), lambda j, i: (i, 0)),
            pl.BlockSpec((pl.Element(tn), pl.Element(k)),
                         lambda j, i: (pl.multiple_of(row0 + j * tn, SUBLANES), 0)),
        ],
        out_specs=pl.BlockSpec((tm, tn), lambda j, i: (i, j)),
        scratch_shapes=[pltpu.VMEM((tn, k), BF16)],
        compiler_params=_params("parallel", "arbitrary"),
        name=name,
    )(a, w_t)


def _swa_kernel(sink_ref, q_ref, kc_ref, kp_ref, vc_ref, vp_ref, o_ref, *, blocks_per_seq):
    w = WINDOW
    dh = HEAD_DIM_A
    slot = lax.broadcasted_iota(jnp.int32, (w, w), 0)
    qry = lax.broadcasted_iota(jnp.int32, (w, w), 1)
    from_prev = slot > qry
    prev_bias = jnp.where((pl.program_id(0) % blocks_per_seq) > 0, 0.0, -jnp.inf)
    lo = lax.broadcasted_iota(jnp.int32, (2 * w, LANES), 1) < dh
    zero_k = jnp.zeros((2 * w, LANES), BF16)
    zero_v = jnp.zeros((dh, 2 * w), BF16)

    v_bdts = []
    windows = []
    for pair in range(HKV_A // 2):
        ls = slice(pair * LANES, (pair + 1) * LANES)
        kt = jnp.concatenate([kp_ref[:, ls], kc_ref[:, ls]], axis=0)
        kt_r = pltpu.roll(kt.astype(F32), dh, 1).astype(BF16)
        vt_t = jnp.concatenate([vp_ref[:, ls], vc_ref[:, ls]], axis=0).astype(F32).T.astype(BF16)
        for e in range(2):
            hkv = 2 * pair + e
            k_lo, k_hi = (kt, kt_r) if e == 0 else (kt_r, kt)
            k_bd = jnp.concatenate([jnp.where(lo, k_lo, zero_k), jnp.where(lo, zero_k, k_hi)], axis=0)
            v_t = vt_t[e * dh:(e + 1) * dh, :]
            v_bdts.append(jnp.concatenate([jnp.concatenate([v_t, zero_v], axis=1),
                                           jnp.concatenate([zero_v, v_t], axis=1)], axis=0))
            q2 = jnp.concatenate([q_ref[:, (2 * hkv + t) * LANES:(2 * hkv + t + 1) * LANES] for t in range(2)],
                                 axis=0)
            s_t = _dot_nt(k_bd, q2)
            for t in range(2):
                for hh in range(2):
                    blk = s_t[2 * w * hh:2 * w * (hh + 1), t * w:(t + 1) * w]
                    windows.append(jnp.where(from_prev, blk[:w, :] + prev_bias, blk[w:, :]))
    s_all = jnp.concatenate(windows, axis=1)
    sink = jnp.concatenate([jnp.full((1, w), sink_ref[n], F32) for n in range(HQ_A)], axis=1)
    m = jnp.maximum(jnp.max(s_all, axis=0, keepdims=True), sink)
    p = jnp.exp(s_all - m)
    denom = jnp.sum(p, axis=0, keepdims=True) + jnp.exp(sink - m)
    p = p * (1.0 / denom)

    zero_p = jnp.zeros((w, w), F32)
    for hkv in range(HKV_A):
        cols = []
        for t in range(2):
            parts = []
            for hh in range(2):
                n = 4 * hkv + 2 * t + hh
                ph = p[:, n * w:(n + 1) * w]
                parts += [jnp.where(from_prev, ph, zero_p), jnp.where(from_prev, zero_p, ph)]
            cols.append(jnp.concatenate(parts, axis=0))
        probs_t = jnp.concatenate(cols, axis=1).astype(BF16)
        o_t = _dot(v_bdts[hkv], probs_t)
        for t in range(2):
            tile = 2 * hkv + t
            o_ref[:, tile * LANES:(tile + 1) * LANES] = o_t[:, t * w:(t + 1) * w].T.astype(o_ref.dtype)


def _swa(qkv, sinks, seq):
    t = qkv.shape[0]
    w = WINDOW
    nb = seq // w
    q_blocks = Q_A // KV_A

    def prev(i):
        return jnp.maximum(i - 1, 0)

    return pl.pallas_call(
        functools.partial(_swa_kernel, blocks_per_seq=nb),
        out_shape=jax.ShapeDtypeStruct((t, Q_A), BF16),
        grid=(t // w,),
        in_specs=[
            pl.BlockSpec(memory_space=pltpu.SMEM),
            pl.BlockSpec((w, Q_A), lambda i: (i, 0)),
            pl.BlockSpec((w, KV_A), lambda i: (i, q_blocks)),
            pl.BlockSpec((w, KV_A), lambda i: (prev(i), q_blocks)),
            pl.BlockSpec((w, KV_A), lambda i: (i, q_blocks + 1)),
            pl.BlockSpec((w, KV_A), lambda i: (prev(i), q_blocks + 1)),
        ],
        out_specs=pl.BlockSpec((w, Q_A), lambda i: (i, 0)),
        compiler_params=_params("parallel"),
        name="swa",
    )(sinks, qkv, qkv, qkv, qkv, qkv)


def _gla_kernel(q_ref, k_ref, v_ref, og_ref, lr_ref, wup_ref, bgk_ref, gn_ref, o_ref, st_ref):
    @pl.when(pl.program_id(1) == 0)
    def _():
        st_ref[...] = jnp.zeros_like(st_ref)

    c = CHUNK
    rows_blk = q_ref.shape[0]
    nch = rows_blk // c
    mid = c // 2
    z = _dot(lr_ref[...], wup_ref[...]) + bgk_ref[...]
    g = jax.nn.log_sigmoid(z) / GK_NORMALIZER
    g1 = g.astype(BF16)
    rem = g - g1.astype(F32)
    g2 = rem.astype(BF16)
    g3 = (rem - g2.astype(F32)).astype(BF16)

    def sums(mat):
        return _dot(mat, g1) + _dot(mat, g2) + _dot(mat, g3)

    shift = c.bit_length() - 1
    r_i = lax.broadcasted_iota(jnp.int32, (rows_blk, rows_blk), 0)
    c_i = lax.broadcasted_iota(jnp.int32, (rows_blk, rows_blk), 1)
    same = (r_i >> shift) == (c_i >> shift)
    r_in = r_i & (c - 1)
    c_in = c_i & (c - 1)
    plus = same & (c_in <= r_in) & (c_in > mid)
    minus = same & (c_in > r_in) & (c_in <= mid)
    d_mid = jnp.where(plus, 1.0, 0.0) - jnp.where(minus, 1.0, 0.0)
    b_rel = sums(d_mid.astype(BF16))
    sel_rows = 2 * SUBLANES
    s_r = lax.broadcasted_iota(jnp.int32, (sel_rows, rows_blk), 0)
    s_c = lax.broadcasted_iota(jnp.int32, (sel_rows, rows_blk), 1)
    s_chunk = s_c >> shift
    pick = ((s_r == s_chunk) & ((s_c & (c - 1)) <= mid)) | (s_r == s_chunk + nch)
    b_sel = sums(jnp.where(pick, 1.0, 0.0).astype(BF16))

    q = q_ref[...].astype(F32) * (DK_B ** -0.5)
    k = k_ref[...].astype(F32)
    qs = q * jnp.exp(b_rel)
    ks = k * jnp.exp(-b_rel)
    qs_b = qs.astype(BF16)
    ks_b = ks.astype(BF16)
    gn = gn_ref[...]
    cr_i = lax.broadcasted_iota(jnp.int32, (c, c), 0)
    cc_i = lax.broadcasted_iota(jnp.int32, (c, c), 1)
    causal = cr_i >= cc_i
    for ch in range(nch):
        rows = slice(ch * c, (ch + 1) * c)
        b_mid = b_sel[ch:ch + 1, :]
        b_last = b_sel[nch + ch:nch + ch + 1, :]
        q_in = (qs[rows, :] * jnp.exp(b_mid)).astype(BF16)
        k_st = (ks[rows, :] * jnp.exp(b_last - b_mid)).astype(BF16)
        decay = jnp.exp(b_last)
        for h in range(H_B):
            sk = slice(h * DK_B, (h + 1) * DK_B)
            sv = slice(h * DV_B, (h + 1) * DV_B)
            v = v_ref[rows, sv]
            a = jnp.where(causal, _dot_nt(qs_b[rows, sk], ks_b[rows, sk]), 0.0)
            st = st_ref[h]
            o = _dot(a.astype(BF16), v) + _dot_nt(q_in[:, sk], st.astype(BF16))
            st_ref[h] = st * decay[:, sk] + _dot_tn(v, k_st[:, sk])
            ms = jnp.mean(o * o, axis=-1, keepdims=True)
            og = og_ref[rows, sv].astype(F32)
            out = (o * lax.rsqrt(ms + EPS) * gn) * (og * jax.nn.sigmoid(og))
            o_ref[rows, sv] = out.astype(o_ref.dtype)


def _gla(pb, og, lr, w_up_pad, b_gk, gla_norm, batch, seq):
    t = pb.shape[0]
    c = ROWS_GLA
    nc = seq // c

    def rows(b, s):
        return b * nc + s

    return pl.pallas_call(
        _gla_kernel,
        out_shape=jax.ShapeDtypeStruct((t, V_B), BF16),
        grid=(batch, nc),
        in_specs=[
            pl.BlockSpec((c, QK_B), lambda b, s: (rows(b, s), 0)),
            pl.BlockSpec((c, QK_B), lambda b, s: (rows(b, s), 1)),
            pl.BlockSpec((c, V_B), lambda b, s: (rows(b, s), 1)),
            pl.BlockSpec((c, V_B), lambda b, s: (rows(b, s), 0)),
            pl.BlockSpec((c, GK_PAD), lambda b, s: (rows(b, s), 0)),
            pl.BlockSpec((GK_PAD, QK_B), lambda b, s: (0, 0)),
            pl.BlockSpec((1, QK_B), lambda b, s: (0, 0)),
            pl.BlockSpec((1, DV_B), lambda b, s: (0, 0)),
        ],
        out_specs=pl.BlockSpec((c, V_B), lambda b, s: (rows(b, s), 0)),
        scratch_shapes=[pltpu.VMEM((H_B, DV_B, DK_B), F32)],
        compiler_params=_params("parallel", "arbitrary"),
        name="gla",
    )(pb, pb, pb, og, lr, w_up_pad, b_gk, gla_norm)


def _merge_kernel(oa_ref, ob_ref, wa_ref, wb_ref, ga_ref, gb_ref, o_ref):
    ta = _dot(oa_ref[...], wa_ref[...])
    tb = _dot(ob_ref[...], wb_ref[...])
    out = ga_ref[...].astype(F32) * ta + gb_ref[...].astype(F32) * tb
    o_ref[...] = out.astype(o_ref.dtype)


def _merge(o_a, o_b, w_a, w_b, gates):
    t = o_a.shape[0]
    d = w_a.shape[1]
    tm, tn = TM_MERGE, TN_MERGE
    nj = d // tn
    return pl.pallas_call(
        _merge_kernel,
        out_shape=jax.ShapeDtypeStruct((t, d), BF16),
        grid=(t // tm, nj),
        in_specs=[
            pl.BlockSpec((tm, Q_A), lambda i, j: (i, 0)),
            pl.BlockSpec((tm, V_B), lambda i, j: (i, 0)),
            pl.BlockSpec((Q_A, tn), lambda i, j: (0, j)),
            pl.BlockSpec((V_B, tn), lambda i, j: (0, j)),
            pl.BlockSpec((tm, tn), lambda i, j: (i, j)),
            pl.BlockSpec((tm, tn), lambda i, j: (i, j + nj)),
        ],
        out_specs=pl.BlockSpec((tm, tn), lambda i, j: (i, j)),
        compiler_params=_params("parallel", "parallel"),
        name="merge",
    )(o_a, o_b, w_a, w_b, gates, gates)


def _rms_residual(x, gate, y, nw):
    ms = jnp.mean(y * y, axis=-1, keepdims=True)
    return x + gate * (y * lax.rsqrt(ms + EPS) * nw)


def _out_proj_kernel(a_ref, w_ref, x_ref, g_ref, nw_ref, nw2_ref, sc2_ref, sh2_ref, o_ref, h_ref):
    for r in range(0, a_ref.shape[0], ROWS_EPILOGUE):
        rows = slice(r, r + ROWS_EPILOGUE)
        y = _dot(a_ref[rows, :], w_ref[...])
        x1 = _rms_residual(x_ref[rows, :], g_ref[...], y, nw_ref[...])
        o_ref[rows, :] = x1
        ms = jnp.mean(x1 * x1, axis=-1, keepdims=True)
        h2 = x1 * lax.rsqrt(ms + EPS) * nw2_ref[...]
        h_ref[rows, :] = (h2 * (1.0 + sc2_ref[...]) + sh2_ref[...]).astype(h_ref.dtype)


def _out_proj(a, w, x2, gate, nw, nw2, sc2, sh2, seq):
    t, k = a.shape
    d = w.shape[1]
    tm = TM_OUT
    per_seq = seq // tm
    row = pl.BlockSpec((1, d), lambda i: (0, 0))
    per_batch = pl.BlockSpec((None, 1, d), lambda i: (i // per_seq, 0, 0))
    return pl.pallas_call(
        _out_proj_kernel,
        out_shape=(jax.ShapeDtypeStruct((t, d), F32), jax.ShapeDtypeStruct((t, d), BF16)),
        grid=(t // tm,),
        in_specs=[
            pl.BlockSpec((tm, k), lambda i: (i, 0)),
            pl.BlockSpec((k, d), lambda i: (0, 0)),
            pl.BlockSpec((tm, d), lambda i: (i, 0)),
            per_batch, row, row, per_batch, per_batch,
        ],
        out_specs=(pl.BlockSpec((tm, d), lambda i: (i, 0)), pl.BlockSpec((tm, d), lambda i: (i, 0))),
        compiler_params=_params("parallel"),
        name="out_proj",
    )(a, w, x2, gate, nw, nw2, sc2, sh2)


def _ffn_kernel(h_ref, wg_ref, wv_ref, cwg_ref, cwv_ref, cbg_ref, cbv_ref, wd_ref, x_hbm, g_ref, nw_ref,
                o_ref, x_ref, x_sem, act0_ref, act1_ref, carry_ref, *, nj, tiles_per_seq):
    i = pl.program_id(0)
    j = pl.program_id(1)
    tm = h_ref.shape[0]
    tn = wg_ref.shape[1]
    seq_start = (i % tiles_per_seq) == 0

    def x_copy():
        return pltpu.make_async_copy(x_hbm.at[pl.ds(pl.multiple_of(i * tm, tm), tm), :], x_ref, x_sem)

    def up(act_ref):
        h = h_ref[...]
        jc = jnp.minimum(j, nj - 1)
        tails = []

        def conv(u, cols, cw_ref, cb_ref):
            row = lax.broadcasted_iota(jnp.int32, u.shape, 0)
            c6 = jnp.where(seq_start, 0.0, carry_ref[jc, SUBLANES - 2:SUBLANES - 1, cols])
            c7 = jnp.where(seq_start, 0.0, carry_ref[jc, SUBLANES - 1:SUBLANES, cols])
            u1 = jnp.where(row == 0, c7, pltpu.roll(u, 1, 0))
            u2 = jnp.where(row == 0, c6, jnp.where(row == 1, c7, pltpu.roll(u, 2, 0)))
            tails.append((cols, u[tm - SUBLANES:, :]))
            return cw_ref[0:1, :] * u2 + cw_ref[1:2, :] * u1 + cw_ref[2:3, :] * u + cb_ref[...]

        gate = conv(_dot(h, wg_ref[...]), slice(0, tn), cwg_ref, cbg_ref)
        val = conv(_dot(h, wv_ref[...]), slice(tn, 2 * tn), cwv_ref, cbv_ref)
        act_ref[...] = (gate * jax.nn.sigmoid(gate) * val).astype(act_ref.dtype)
        return jc, tails

    def save_carry(jc, tails):
        for cols, tail in tails:
            carry_ref[jc, :, cols] = tail

    def down(act_ref):
        o_ref[...] += _dot(act_ref[...], wd_ref[...])

    @pl.when(j == 0)
    def _():
        x_copy().start()
        o_ref[...] = jnp.zeros_like(o_ref)
        save_carry(*up(act0_ref))

    @pl.when((j > 0) & (j < nj) & (j % 2 == 1))
    def _():
        jc, tails = up(act1_ref)
        down(act0_ref)
        save_carry(jc, tails)

    @pl.when((j > 0) & (j < nj) & (j % 2 == 0))
    def _():
        jc, tails = up(act0_ref)
        down(act1_ref)
        save_carry(jc, tails)

    @pl.when(j == nj)
    def _():
        last = act0_ref if (nj - 1) % 2 == 0 else act1_ref
        down(last)
        x_copy().wait()
        for r in range(0, tm, ROWS_EPILOGUE):
            rows = slice(r, r + ROWS_EPILOGUE)
            o_ref[rows, :] = _rms_residual(x_ref[rows, :], g_ref[...], o_ref[rows, :], nw_ref[...])


def _ffn(h, w_up, conv_w, conv_b, w_down, x2, gate, nw, seq):
    t, d = h.shape
    tm, tn = TM_FFN, TN_FFN
    nj = D_FF // tn
    tiles_per_seq = seq // tm

    def up_tile(j):
        return jnp.minimum(j, nj - 1)

    def down_tile(j):
        return jnp.maximum(j - 1, 0)

    return pl.pallas_call(
        functools.partial(_ffn_kernel, nj=nj, tiles_per_seq=tiles_per_seq),
        out_shape=jax.ShapeDtypeStruct((t, d), F32),
        grid=(t // tm, nj + 1),
        in_specs=[
            pl.BlockSpec((tm, d), lambda i, j: (i, 0)),
            pl.BlockSpec((None, d, tn), lambda i, j: (up_tile(j), 0, 0)),
            pl.BlockSpec((None, d, tn), lambda i, j: (up_tile(j) + nj, 0, 0)),
            pl.BlockSpec((CONV_WIDTH, tn), lambda i, j: (0, up_tile(j))),
            pl.BlockSpec((CONV_WIDTH, tn), lambda i, j: (0, up_tile(j) + nj)),
            pl.BlockSpec((1, tn), lambda i, j: (0, up_tile(j))),
            pl.BlockSpec((1, tn), lambda i, j: (0, up_tile(j) + nj)),
            pl.BlockSpec((tn, d), lambda i, j: (down_tile(j), 0)),
            pl.BlockSpec(memory_space=pl.ANY),
            pl.BlockSpec((None, 1, d), lambda i, j: (i // tiles_per_seq, 0, 0)),
            pl.BlockSpec((1, d), lambda i, j: (0, 0)),
        ],
        out_specs=pl.BlockSpec((tm, d), lambda i, j: (i, 0)),
        scratch_shapes=[
            pltpu.VMEM((tm, d), F32),
            pltpu.SemaphoreType.DMA,
            pltpu.VMEM((tm, tn), BF16),
            pltpu.VMEM((tm, tn), BF16),
            pltpu.VMEM((nj, SUBLANES, 2 * tn), F32),
        ],
        compiler_params=_params("arbitrary", "arbitrary", vmem=VMEM_LIMIT_FFN),
        name="ffn",
    )(h, w_up, w_up, conv_w, conv_w, conv_b, conv_b, w_down, x2, gate, nw)


def kernel(x, c, positions, w_mod, b_mod, mix_norm_pre, mix_norm_post, w_in, attn_sinks, w_gk_up, b_gk, gla_norm, w_branch_attn, w_branch_gla, w_out, ffn_norm_pre, ffn_norm_post, w_up, conv_w, conv_b, w_down):
    batch, seq, d = x.shape
    t = batch * seq
    depth = w_mod.shape[0]
    x2 = x.reshape(t, d)

    half = HEAD_DIM_A // 2
    inv_freq = ROPE_THETA ** (-jnp.arange(0, HEAD_DIM_A, 2, dtype=F32) / HEAD_DIM_A)
    invf_row = jnp.tile(inv_freq, LANES // half).reshape(1, LANES)
    sgn_row = jnp.tile(jnp.concatenate([-jnp.ones((half,), F32), jnp.ones((half,), F32)]),
                       LANES // HEAD_DIM_A).reshape(1, LANES)
    cos_t, sin_t = _rope_tables(positions.reshape(t, 1), invf_row, sgn_row)

    c_pad = jnp.zeros((SUBLANES, d), F32).at[:batch].set(c)

    o_qb = Q_A + 2 * KV_A
    o_lr = o_qb + 2 * QK_B + V_B
    o_og = o_lr + GK_RANK

    for l in range(depth):
        mod = _modulation(c_pad, w_mod[l], b_mod[l].reshape(1, -1))[:batch]
        sh1, sc1, g1, sh2, sc2, g2 = [m.reshape(batch, 1, d) for m in jnp.split(mod, 6, axis=-1)]

        w_in_t = w_in[l].T
        w_gk_pad = jnp.zeros((GK_PAD, QK_B), BF16).at[:GK_RANK].set(w_gk_up[l].astype(BF16))
        nj = D_FF // TN_FFN
        w_up_tiles = w_up[l].astype(BF16).reshape(d, 2 * nj, TN_FFN).transpose(1, 0, 2)

        h = _norm_mod(x2, mix_norm_pre[l].reshape(1, d), sc1, sh1, seq)
        qkv_a, gk_lr = _proj_a(h, w_in_t, o_lr, cos_t, sin_t)
        pb = _matmul(h, w_in_t, row0=o_qb, n=o_lr - o_qb, act=None, name="proj_b")
        og = _matmul(h, w_in_t, row0=o_og, n=V_B, act=None, name="proj_og")
        gates = _matmul(h, w_in_t, row0=o_og + V_B, n=2 * d, act="sigmoid", name="gates")
        o_a = _swa(qkv_a, attn_sinks[l], seq)
        o_b = _gla(pb, og, gk_lr, w_gk_pad, b_gk[l].reshape(1, -1), gla_norm[l].reshape(1, -1), batch, seq)
        merged = _merge(o_a, o_b, w_branch_attn[l].astype(BF16), w_branch_gla[l].astype(BF16), gates)
        x2, h = _out_proj(merged, w_out[l].astype(BF16), x2, g1, mix_norm_post[l].reshape(1, d),
                          ffn_norm_pre[l].reshape(1, d), sc2, sh2, seq)

        x2 = _ffn(h, w_up_tiles, conv_w[l], conv_b[l].reshape(1, -1), w_down[l].astype(BF16),
                  x2, g2, ffn_norm_post[l].reshape(1, d), seq)

    return x2.reshape(batch, seq, d)
```

```python
import functools

import jax
import jax.numpy as jnp
from jax import lax
from jax.experimental import pallas as pl
from jax.experimental.pallas import tpu as pltpu

F32 = jnp.float32
BF16 = jnp.bfloat16

D_MODEL = 2048
HEAD_DIM_A = 64
HQ_A = 16
HKV_A = 4
WINDOW = 128
ROPE_THETA = 10000.0
H_B = 4
DK_B = 256
DV_B = 512
GK_RANK = 16
GK_NORMALIZER = 16.0
CHUNK = 64
D_FF = 5632
CONV_WIDTH = 3
EPS = 1e-6

Q_A = HQ_A * HEAD_DIM_A
KV_A = HKV_A * HEAD_DIM_A
QK_B = H_B * DK_B
V_B = H_B * DV_B
QKV_A = Q_A + 2 * KV_A
ROPE_COLS = Q_A + KV_A

LANES = 128
SUBLANES = 8
GK_PAD = LANES
VMEM_LIMIT = 48 * 1024 * 1024
VMEM_LIMIT_FFN = 56 * 1024 * 1024

TM_NORM = 512
TM_ROPE = 2048
TM_PROJ_A = 512
TM_MM, TN_MM = 1024, 1024
LR_ROWS = 16
TM_MERGE, TN_MERGE = 1024, 512
TM_OUT = 512
TM_FFN, TN_FFN = 1024, 512
ROWS_GLA = 256
BLOCKS_SWA = 4
ROWS_EPILOGUE = 256
TN_MOD = 1024


def _params(*sem, vmem=VMEM_LIMIT):
    return pltpu.CompilerParams(dimension_semantics=sem, vmem_limit_bytes=vmem)


def _dot(a, b):
    return jnp.dot(a, b, preferred_element_type=F32)


def _dot_nt(a, b):
    return lax.dot_general(a, b, (((1,), (1,)), ((), ())), preferred_element_type=F32)


def _dot_tn(a, b):
    return lax.dot_general(a, b, (((0,), (0,)), ((), ())), preferred_element_type=F32)


def _mod_kernel(c_ref, w_ref, b_ref, o_ref):
    c = c_ref[...]
    c_act = (c * jax.nn.sigmoid(c)).astype(BF16)
    o_ref[...] = _dot(c_act, w_ref[...].astype(BF16)) + b_ref[...]


def _modulation(c_pad, w_mod, b_mod):
    rows, d = c_pad.shape
    n = w_mod.shape[1]
    return pl.pallas_call(
        _mod_kernel,
        out_shape=jax.ShapeDtypeStruct((rows, n), F32),
        grid=(n // TN_MOD,),
        in_specs=[
            pl.BlockSpec((rows, d), lambda j: (0, 0)),
            pl.BlockSpec((d, TN_MOD), lambda j: (0, j)),
            pl.BlockSpec((1, TN_MOD), lambda j: (0, j)),
        ],
        out_specs=pl.BlockSpec((rows, TN_MOD), lambda j: (0, j)),
        compiler_params=_params("parallel"),
        name="mod",
    )(c_pad, w_mod, b_mod)


def _retile_kernel(wg_ref, wv_ref, o_ref):
    tn = wg_ref.shape[1]
    o_ref[:, :tn] = wg_ref[...].astype(o_ref.dtype)
    o_ref[:, tn:] = wv_ref[...].astype(o_ref.dtype)


def _retile_w_up(w_up):
    d = w_up.shape[0]
    tn = TN_FFN
    nj = D_FF // tn
    return pl.pallas_call(
        _retile_kernel,
        out_shape=jax.ShapeDtypeStruct((nj, d, 2 * tn), BF16),
        grid=(nj,),
        in_specs=[
            pl.BlockSpec((d, tn), lambda j: (0, j)),
            pl.BlockSpec((d, tn), lambda j: (0, j + nj)),
        ],
        out_specs=pl.BlockSpec((None, d, 2 * tn), lambda j: (j, 0, 0)),
        compiler_params=_params("parallel"),
        name="retile_w_up",
    )(w_up, w_up)


def _norm_mod_kernel(x_ref, nw_ref, sc_ref, sh_ref, o_ref):
    x = x_ref[...]
    ms = jnp.mean(x * x, axis=-1, keepdims=True)
    y = x * lax.rsqrt(ms + EPS) * nw_ref[...]
    o_ref[...] = (y * (1.0 + sc_ref[...]) + sh_ref[...]).astype(o_ref.dtype)


def _norm_mod(x2, nw, sc, sh, seq):
    t, d = x2.shape
    per_seq = seq // TM_NORM
    return pl.pallas_call(
        _norm_mod_kernel,
        out_shape=jax.ShapeDtypeStruct((t, d), BF16),
        grid=(t // TM_NORM,),
        in_specs=[
            pl.BlockSpec((TM_NORM, d), lambda i: (i, 0)),
            pl.BlockSpec((1, d), lambda i: (0, 0)),
            pl.BlockSpec((None, 1, d), lambda i: (i // per_seq, 0, 0)),
            pl.BlockSpec((None, 1, d), lambda i: (i // per_seq, 0, 0)),
        ],
        out_specs=pl.BlockSpec((TM_NORM, d), lambda i: (i, 0)),
        compiler_params=_params("parallel"),
        name="norm_mod",
    )(x2, nw, sc, sh)


def _rope_table_kernel(pos_ref, invf_ref, sgn_ref, cos_ref, sin_ref):
    ang = pos_ref[...].astype(F32) * invf_ref[...]
    cos_ref[...] = jnp.cos(ang)
    sin_ref[...] = jnp.sin(ang) * sgn_ref[...]


def _rope_tables(pos_col, invf_row, sgn_row):
    t = pos_col.shape[0]
    return pl.pallas_call(
        _rope_table_kernel,
        out_shape=(jax.ShapeDtypeStruct((t, LANES), F32), jax.ShapeDtypeStruct((t, LANES), F32)),
        grid=(t // TM_ROPE,),
        in_specs=[
            pl.BlockSpec((TM_ROPE, 1), lambda i: (i, 0)),
            pl.BlockSpec((1, LANES), lambda i: (0, 0)),
            pl.BlockSpec((1, LANES), lambda i: (0, 0)),
        ],
        out_specs=(
            pl.BlockSpec((TM_ROPE, LANES), lambda i: (i, 0)),
            pl.BlockSpec((TM_ROPE, LANES), lambda i: (i, 0)),
        ),
        compiler_params=_params("parallel"),
        name="rope_tables",
    )(pos_col, invf_row, sgn_row)


def _proj_a_kernel(h_ref, w_ref, wlr_ref, cos_ref, sin_ref, qkv_ref, lr_ref, wbf_ref):
    @pl.when(pl.program_id(0) == 0)
    def _():
        wbf_ref[:QKV_A, :] = w_ref[...].astype(BF16)
        wbf_ref[QKV_A:QKV_A + LR_ROWS, :] = wlr_ref[...].astype(BF16)
        wbf_ref[QKV_A + LR_ROWS:, :] = jnp.zeros((GK_PAD - LR_ROWS, wbf_ref.shape[1]), BF16)

    acc = _dot_nt(h_ref[...], wbf_ref[...])
    cos = cos_ref[...]
    sin = sin_ref[...]
    lane = lax.broadcasted_iota(jnp.int32, cos.shape, 1)
    first_half = (lane % HEAD_DIM_A) < (HEAD_DIM_A // 2)
    half = HEAD_DIM_A // 2
    for c in range(ROPE_COLS // LANES):
        y = acc[:, c * LANES:(c + 1) * LANES]
        partner = jnp.where(first_half, pltpu.roll(y, LANES - half, 1), pltpu.roll(y, half, 1))
        r = y * cos + partner * sin
        if c < Q_A // LANES:
            r = r * (HEAD_DIM_A ** -0.5)
        qkv_ref[:, c * LANES:(c + 1) * LANES] = r.astype(qkv_ref.dtype)
    qkv_ref[:, ROPE_COLS:QKV_A] = acc[:, ROPE_COLS:QKV_A].astype(qkv_ref.dtype)
    lr_ref[...] = acc[:, QKV_A:QKV_A + GK_PAD].astype(lr_ref.dtype)


def _proj_a(h, w_in_t, lr_row0, cos_t, sin_t):
    t, d = h.shape
    tm = TM_PROJ_A
    once = pl.Buffered(1)
    return pl.pallas_call(
        _proj_a_kernel,
        out_shape=(jax.ShapeDtypeStruct((t, QKV_A), BF16), jax.ShapeDtypeStruct((t, GK_PAD), BF16)),
        grid=(t // tm,),
        in_specs=[
            pl.BlockSpec((tm, d), lambda i: (i, 0)),
            pl.BlockSpec((QKV_A, d), lambda i: (0, 0), pipeline_mode=once),
            pl.BlockSpec((pl.Element(LR_ROWS), pl.Element(d)), lambda i: (lr_row0, 0), pipeline_mode=once),
            pl.BlockSpec((tm, LANES), lambda i: (i, 0)),
            pl.BlockSpec((tm, LANES), lambda i: (i, 0)),
        ],
        out_specs=(
            pl.BlockSpec((tm, QKV_A), lambda i: (i, 0)),
            pl.BlockSpec((tm, GK_PAD), lambda i: (i, 0)),
        ),
        scratch_shapes=[pltpu.VMEM((QKV_A + GK_PAD, d), BF16)],
        compiler_params=_params("arbitrary"),
        name="proj_a",
    )(h, w_in_t, w_in_t, cos_t, sin_t)


def _mm_kernel(a_ref, wt_ref, o_ref, wbf_ref, *, act):
    @pl.when(pl.program_id(1) == 0)
    def _():
        wbf_ref[...] = wt_ref[...].astype(BF16)

    acc = _dot_nt(a_ref[...], wbf_ref[...])
    if act == "sigmoid":
        acc = jax.nn.sigmoid(acc)
    o_ref[...] = acc.astype(o_ref.dtype)


def _matmul(a, w_t, *, row0, n, act, name):
    m, k = a.shape
    tm, tn = TM_MM, TN_MM
    assert row0 % SUBLANES == 0 and n % tn == 0
    return pl.pallas_call(
        functools.partial(_mm_kernel, act=act),
        out_shape=jax.ShapeDtypeStruct((m, n), BF16),
        grid=(n // tn, m // tm),
        in_specs=[
            pl.BlockSpec((tm, k), lambda j, i: (i, 0)),
            pl.BlockSpec((pl.Element(tn), pl.Element(k)),
                         lambda j, i: (pl.multiple_of(row0 + j * tn, SUBLANES), 0)),
        ],
        out_specs=pl.BlockSpec((tm, tn), lambda j, i: (i, j)),
        scratch_shapes=[pltpu.VMEM((tn, k), BF16)],
        compiler_params=_params("parallel", "arbitrary"),
        name=name,
    )(a, w_t)


def _swa_kernel(sink_ref, q_ref, kc_ref, kp_ref, vc_ref, vp_ref, o_ref, *, blocks_per_seq):
    w = WINDOW
    for blk in range(q_ref.shape[0] // w):
        rows = slice(blk * w, (blk + 1) * w)
        if blk == 0:
            first = (pl.program_id(0) * (q_ref.shape[0] // w)) % blocks_per_seq == 0
            prev_bias = jnp.where(first, -jnp.inf, 0.0)
            kp, vp = kp_ref, vp_ref
        else:
            prev_bias = 0.0
            prev_rows = slice((blk - 1) * w, blk * w)
            kp, vp = kc_ref.at[prev_rows, :], vc_ref.at[prev_rows, :]
        _swa_block(sink_ref, q_ref.at[rows, :], kc_ref.at[rows, :], kp, vc_ref.at[rows, :], vp,
                   o_ref.at[rows, :], prev_bias)


def _swa_block(sink_ref, q_ref, kc_ref, kp_ref, vc_ref, vp_ref, o_ref, prev_bias):
    w = WINDOW
    dh = HEAD_DIM_A
    slot = lax.broadcasted_iota(jnp.int32, (w, w), 0)
    qry = lax.broadcasted_iota(jnp.int32, (w, w), 1)
    from_prev = slot > qry
    lo = lax.broadcasted_iota(jnp.int32, (2 * w, LANES), 1) < dh
    zero_k = jnp.zeros((2 * w, LANES), BF16)
    zero_v = jnp.zeros((dh, 2 * w), BF16)

    v_bdts = []
    windows = []
    for pair in range(HKV_A // 2):
        ls = slice(pair * LANES, (pair + 1) * LANES)
        kt = jnp.concatenate([kp_ref[:, ls], kc_ref[:, ls]], axis=0)
        kt_r = pltpu.roll(kt.astype(F32), dh, 1).astype(BF16)
        vt_t = jnp.concatenate([vp_ref[:, ls], vc_ref[:, ls]], axis=0).astype(F32).T.astype(BF16)
        for e in range(2):
            hkv = 2 * pair + e
            k_lo, k_hi = (kt, kt_r) if e == 0 else (kt_r, kt)
            k_bd = jnp.concatenate([jnp.where(lo, k_lo, zero_k), jnp.where(lo, zero_k, k_hi)], axis=0)
            v_t = vt_t[e * dh:(e + 1) * dh, :]
            v_bdts.append(jnp.concatenate([jnp.concatenate([v_t, zero_v], axis=1),
                                           jnp.concatenate([zero_v, v_t], axis=1)], axis=0))
            q2 = jnp.concatenate([q_ref[:, (2 * hkv + t) * LANES:(2 * hkv + t + 1) * LANES] for t in range(2)],
                                 axis=0)
            s_t = _dot_nt(k_bd, q2)
            for t in range(2):
                for hh in range(2):
                    blk = s_t[2 * w * hh:2 * w * (hh + 1), t * w:(t + 1) * w]
                    windows.append(jnp.where(from_prev, blk[:w, :] + prev_bias, blk[w:, :]))
    s_all = jnp.concatenate(windows, axis=1)
    sink = jnp.concatenate([jnp.full((1, w), sink_ref[n], F32) for n in range(HQ_A)], axis=1)
    m = jnp.maximum(jnp.max(s_all, axis=0, keepdims=True), sink)
    p = jnp.exp(s_all - m)
    denom = jnp.sum(p, axis=0, keepdims=True) + jnp.exp(sink - m)
    p = p * (1.0 / denom)

    zero_p = jnp.zeros((w, w), F32)
    for hkv in range(HKV_A):
        cols = []
        for t in range(2):
            parts = []
            for hh in range(2):
                n = 4 * hkv + 2 * t + hh
                ph = p[:, n * w:(n + 1) * w]
                parts += [jnp.where(from_prev, ph, zero_p), jnp.where(from_prev, zero_p, ph)]
            cols.append(jnp.concatenate(parts, axis=0))
        probs_t = jnp.concatenate(cols, axis=1).astype(BF16)
        o_t = _dot(v_bdts[hkv], probs_t)
        for t in range(2):
            tile = 2 * hkv + t
            o_ref[:, tile * LANES:(tile + 1) * LANES] = o_t[:, t * w:(t + 1) * w].T.astype(o_ref.dtype)


def _swa(qkv, sinks, seq):
    t = qkv.shape[0]
    w = WINDOW
    nb = seq // w
    q_blocks = Q_A // KV_A

    def prev(i):
        return jnp.maximum(i - 1, 0)

    nblk = BLOCKS_SWA
    rows = nblk * w
    return pl.pallas_call(
        functools.partial(_swa_kernel, blocks_per_seq=nb),
        out_shape=jax.ShapeDtypeStruct((t, Q_A), BF16),
        grid=(t // rows,),
        in_specs=[
            pl.BlockSpec(memory_space=pltpu.SMEM),
            pl.BlockSpec((rows, Q_A), lambda i: (i, 0)),
            pl.BlockSpec((rows, KV_A), lambda i: (i, q_blocks)),
            pl.BlockSpec((w, KV_A), lambda i: (prev(i * nblk), q_blocks)),
            pl.BlockSpec((rows, KV_A), lambda i: (i, q_blocks + 1)),
            pl.BlockSpec((w, KV_A), lambda i: (prev(i * nblk), q_blocks + 1)),
        ],
        out_specs=pl.BlockSpec((rows, Q_A), lambda i: (i, 0)),
        compiler_params=_params("parallel"),
        name="swa",
    )(sinks, qkv, qkv, qkv, qkv, qkv)


def _gla_kernel(q_ref, k_ref, v_ref, og_ref, lr_ref, wup_ref, bgk_ref, gn_ref, o_ref, st_ref):
    @pl.when(pl.program_id(1) == 0)
    def _():
        st_ref[...] = jnp.zeros_like(st_ref)

    c = CHUNK
    rows_blk = q_ref.shape[0]
    nch = rows_blk // c
    mid = c // 2
    z = _dot(lr_ref[...], wup_ref[...]) + bgk_ref[...]
    g = jax.nn.log_sigmoid(z) / GK_NORMALIZER
    g1 = g.astype(BF16)
    rem = g - g1.astype(F32)
    g2 = rem.astype(BF16)
    g3 = (rem - g2.astype(F32)).astype(BF16)

    def sums(mat):
        return _dot(mat, g1) + _dot(mat, g2) + _dot(mat, g3)

    shift = c.bit_length() - 1
    r_i = lax.broadcasted_iota(jnp.int32, (rows_blk, rows_blk), 0)
    c_i = lax.broadcasted_iota(jnp.int32, (rows_blk, rows_blk), 1)
    same = (r_i >> shift) == (c_i >> shift)
    r_in = r_i & (c - 1)
    c_in = c_i & (c - 1)
    plus = same & (c_in <= r_in) & (c_in > mid)
    minus = same & (c_in > r_in) & (c_in <= mid)
    d_mid = jnp.where(plus, 1.0, 0.0) - jnp.where(minus, 1.0, 0.0)
    b_rel = sums(d_mid.astype(BF16))
    sel_rows = 2 * SUBLANES
    s_r = lax.broadcasted_iota(jnp.int32, (sel_rows, rows_blk), 0)
    s_c = lax.broadcasted_iota(jnp.int32, (sel_rows, rows_blk), 1)
    s_chunk = s_c >> shift
    pick = ((s_r == s_chunk) & ((s_c & (c - 1)) <= mid)) | (s_r == s_chunk + nch)
    b_sel = sums(jnp.where(pick, 1.0, 0.0).astype(BF16))

    q = q_ref[...].astype(F32) * (DK_B ** -0.5)
    k = k_ref[...].astype(F32)
    qs = q * jnp.exp(b_rel)
    ks = k * jnp.exp(-b_rel)
    qs_b = qs.astype(BF16)
    ks_b = ks.astype(BF16)
    gn = gn_ref[...]
    cr_i = lax.broadcasted_iota(jnp.int32, (c, c), 0)
    cc_i = lax.broadcasted_iota(jnp.int32, (c, c), 1)
    causal = cr_i >= cc_i
    for ch in range(nch):
        rows = slice(ch * c, (ch + 1) * c)
        b_mid = b_sel[ch:ch + 1, :]
        b_last = b_sel[nch + ch:nch + ch + 1, :]
        q_in = (qs[rows, :] * jnp.exp(b_mid)).astype(BF16)
        k_st = (ks[rows, :] * jnp.exp(b_last - b_mid)).astype(BF16)
        decay = jnp.exp(b_last)
        for h in range(H_B):
            sk = slice(h * DK_B, (h + 1) * DK_B)
            sv = slice(h * DV_B, (h + 1) * DV_B)
            v = v_ref[rows, sv]
            a = jnp.where(causal, _dot_nt(qs_b[rows, sk], ks_b[rows, sk]), 0.0)
            st = st_ref[h]
            o = _dot(a.astype(BF16), v) + _dot_nt(q_in[:, sk], st.astype(BF16))
            st_ref[h] = st * decay[:, sk] + _dot_tn(v, k_st[:, sk])
            ms = jnp.mean(o * o, axis=-1, keepdims=True)
            og = og_ref[rows, sv].astype(F32)
            out = (o * lax.rsqrt(ms + EPS) * gn) * (og * jax.nn.sigmoid(og))
            o_ref[rows, sv] = out.astype(o_ref.dtype)


def _gla(pb, og, lr, w_up_pad, b_gk, gla_norm, batch, seq):
    t = pb.shape[0]
    c = ROWS_GLA
    nc = seq // c

    def rows(b, s):
        return b * nc + s

    return pl.pallas_call(
        _gla_kernel,
        out_shape=jax.ShapeDtypeStruct((t, V_B), BF16),
        grid=(batch, nc),
        in_specs=[
            pl.BlockSpec((c, QK_B), lambda b, s: (rows(b, s), 0)),
            pl.BlockSpec((c, QK_B), lambda b, s: (rows(b, s), 1)),
            pl.BlockSpec((c, V_B), lambda b, s: (rows(b, s), 1)),
            pl.BlockSpec((c, V_B), lambda b, s: (rows(b, s), 0)),
            pl.BlockSpec((c, GK_PAD), lambda b, s: (rows(b, s), 0)),
            pl.BlockSpec((GK_PAD, QK_B), lambda b, s: (0, 0)),
            pl.BlockSpec((1, QK_B), lambda b, s: (0, 0)),
            pl.BlockSpec((1, DV_B), lambda b, s: (0, 0)),
        ],
        out_specs=pl.BlockSpec((c, V_B), lambda b, s: (rows(b, s), 0)),
        scratch_shapes=[pltpu.VMEM((H_B, DV_B, DK_B), F32)],
        compiler_params=_params("parallel", "arbitrary"),
        name="gla",
    )(pb, pb, pb, og, lr, w_up_pad, b_gk, gla_norm)


def _merge_kernel(oa_ref, ob_ref, wa_ref, wb_ref, ga_ref, gb_ref, o_ref):
    ta = _dot(oa_ref[...], wa_ref[...])
    tb = _dot(ob_ref[...], wb_ref[...])
    out = ga_ref[...].astype(F32) * ta + gb_ref[...].astype(F32) * tb
    o_ref[...] = out.astype(o_ref.dtype)


def _merge(o_a, o_b, w_a, w_b, gates):
    t = o_a.shape[0]
    d = w_a.shape[1]
    tm, tn = TM_MERGE, TN_MERGE
    nj = d // tn
    return pl.pallas_call(
        _merge_kernel,
        out_shape=jax.ShapeDtypeStruct((t, d), BF16),
        grid=(t // tm, nj),
        in_specs=[
            pl.BlockSpec((tm, Q_A), lambda i, j: (i, 0)),
            pl.BlockSpec((tm, V_B), lambda i, j: (i, 0)),
            pl.BlockSpec((Q_A, tn), lambda i, j: (0, j)),
            pl.BlockSpec((V_B, tn), lambda i, j: (0, j)),
            pl.BlockSpec((tm, tn), lambda i, j: (i, j)),
            pl.BlockSpec((tm, tn), lambda i, j: (i, j + nj)),
        ],
        out_specs=pl.BlockSpec((tm, tn), lambda i, j: (i, j)),
        compiler_params=_params("parallel", "parallel"),
        name="merge",
    )(o_a, o_b, w_a, w_b, gates, gates)


def _rms_residual(x, gate, y, nw):
    ms = jnp.mean(y * y, axis=-1, keepdims=True)
    return x + gate * (y * lax.rsqrt(ms + EPS) * nw)


def _out_proj_kernel(a_ref, w_ref, x_ref, g_ref, nw_ref, nw2_ref, sc2_ref, sh2_ref, o_ref, h_ref):
    for r in range(0, a_ref.shape[0], ROWS_EPILOGUE):
        rows = slice(r, r + ROWS_EPILOGUE)
        y = _dot(a_ref[rows, :], w_ref[...])
        x1 = _rms_residual(x_ref[rows, :], g_ref[...], y, nw_ref[...])
        o_ref[rows, :] = x1
        ms = jnp.mean(x1 * x1, axis=-1, keepdims=True)
        h2 = x1 * lax.rsqrt(ms + EPS) * nw2_ref[...]
        h_ref[rows, :] = (h2 * (1.0 + sc2_ref[...]) + sh2_ref[...]).astype(h_ref.dtype)


def _out_proj(a, w, x2, gate, nw, nw2, sc2, sh2, seq):
    t, k = a.shape
    d = w.shape[1]
    tm = TM_OUT
    per_seq = seq // tm
    row = pl.BlockSpec((1, d), lambda i: (0, 0))
    per_batch = pl.BlockSpec((None, 1, d), lambda i: (i // per_seq, 0, 0))
    return pl.pallas_call(
        _out_proj_kernel,
        out_shape=(jax.ShapeDtypeStruct((t, d), F32), jax.ShapeDtypeStruct((t, d), BF16)),
        grid=(t // tm,),
        in_specs=[
            pl.BlockSpec((tm, k), lambda i: (i, 0)),
            pl.BlockSpec((k, d), lambda i: (0, 0)),
            pl.BlockSpec((tm, d), lambda i: (i, 0)),
            per_batch, row, row, per_batch, per_batch,
        ],
        out_specs=(pl.BlockSpec((tm, d), lambda i: (i, 0)), pl.BlockSpec((tm, d), lambda i: (i, 0))),
        compiler_params=_params("parallel"),
        name="out_proj",
    )(a, w, x2, gate, nw, nw2, sc2, sh2)


def _ffn_kernel(h_ref, wu_ref, cwg_ref, cwv_ref, cbg_ref, cbv_ref, wd_ref, x_hbm, g_ref, nw_ref,
                o_ref, x_ref, x_sem, act0_ref, act1_ref, carry_ref, *, nj, tiles_per_seq):
    i = pl.program_id(0)
    j = pl.program_id(1)
    tm = h_ref.shape[0]
    tn = wu_ref.shape[1] // 2
    seq_start = (i % tiles_per_seq) == 0

    def x_copy():
        return pltpu.make_async_copy(x_hbm.at[pl.ds(pl.multiple_of(i * tm, tm), tm), :], x_ref, x_sem)

    def up(act_ref):
        h = h_ref[...]
        jc = jnp.minimum(j, nj - 1)
        tails = []

        def conv(u, cols, cw_ref, cb_ref):
            row = lax.broadcasted_iota(jnp.int32, u.shape, 0)
            c6 = jnp.where(seq_start, 0.0, carry_ref[jc, SUBLANES - 2:SUBLANES - 1, cols])
            c7 = jnp.where(seq_start, 0.0, carry_ref[jc, SUBLANES - 1:SUBLANES, cols])
            u1 = jnp.where(row == 0, c7, pltpu.roll(u, 1, 0))
            u2 = jnp.where(row == 0, c6, jnp.where(row == 1, c7, pltpu.roll(u, 2, 0)))
            tails.append((cols, u[tm - SUBLANES:, :]))
            return cw_ref[0:1, :] * u2 + cw_ref[1:2, :] * u1 + cw_ref[2:3, :] * u + cb_ref[...]

        u = _dot(h, wu_ref[...])
        gate = conv(u[:, :tn], slice(0, tn), cwg_ref, cbg_ref)
        val = conv(u[:, tn:], slice(tn, 2 * tn), cwv_ref, cbv_ref)
        act_ref[...] = (gate * jax.nn.sigmoid(gate) * val).astype(act_ref.dtype)
        return jc, tails

    def save_carry(jc, tails):
        for cols, tail in tails:
            carry_ref[jc, :, cols] = tail

    def down(act_ref):
        o_ref[...] += _dot(act_ref[...], wd_ref[...])

    @pl.when(j == 0)
    def _():
        x_copy().start()
        o_ref[...] = jnp.zeros_like(o_ref)
        save_carry(*up(act0_ref))

    @pl.when((j > 0) & (j < nj) & (j % 2 == 1))
    def _():
        jc, tails = up(act1_ref)
        down(act0_ref)
        save_carry(jc, tails)

    @pl.when((j > 0) & (j < nj) & (j % 2 == 0))
    def _():
        jc, tails = up(act0_ref)
        down(act1_ref)
        save_carry(jc, tails)

    @pl.when(j == nj)
    def _():
        last = act0_ref if (nj - 1) % 2 == 0 else act1_ref
        down(last)
        x_copy().wait()
        for r in range(0, tm, ROWS_EPILOGUE):
            rows = slice(r, r + ROWS_EPILOGUE)
            o_ref[rows, :] = _rms_residual(x_ref[rows, :], g_ref[...], o_ref[rows, :], nw_ref[...])


def _ffn(h, w_up, conv_w, conv_b, w_down, x2, gate, nw, seq):
    t, d = h.shape
    tm, tn = TM_FFN, TN_FFN
    nj = D_FF // tn
    tiles_per_seq = seq // tm

    def up_tile(j):
        return jnp.minimum(j, nj - 1)

    def down_tile(j):
        return jnp.maximum(j - 1, 0)

    return pl.pallas_call(
        functools.partial(_ffn_kernel, nj=nj, tiles_per_seq=tiles_per_seq),
        out_shape=jax.ShapeDtypeStruct((t, d), F32),
        grid=(t // tm, nj + 1),
        in_specs=[
            pl.BlockSpec((tm, d), lambda i, j: (i, 0)),
            pl.BlockSpec((None, d, 2 * tn), lambda i, j: (up_tile(j), 0, 0)),
            pl.BlockSpec((CONV_WIDTH, tn), lambda i, j: (0, up_tile(j))),
            pl.BlockSpec((CONV_WIDTH, tn), lambda i, j: (0, up_tile(j) + nj)),
            pl.BlockSpec((1, tn), lambda i, j: (0, up_tile(j))),
            pl.BlockSpec((1, tn), lambda i, j: (0, up_tile(j) + nj)),
            pl.BlockSpec((tn, d), lambda i, j: (down_tile(j), 0)),
            pl.BlockSpec(memory_space=pl.ANY),
            pl.BlockSpec((None, 1, d), lambda i, j: (i // tiles_per_seq, 0, 0)),
            pl.BlockSpec((1, d), lambda i, j: (0, 0)),
        ],
        out_specs=pl.BlockSpec((tm, d), lambda i, j: (i, 0)),
        scratch_shapes=[
            pltpu.VMEM((tm, d), F32),
            pltpu.SemaphoreType.DMA,
            pltpu.VMEM((tm, tn), BF16),
            pltpu.VMEM((tm, tn), BF16),
            pltpu.VMEM((nj, SUBLANES, 2 * tn), F32),
        ],
        compiler_params=_params("arbitrary", "arbitrary", vmem=VMEM_LIMIT_FFN),
        name="ffn",
    )(h, w_up, conv_w, conv_w, conv_b, conv_b, w_down, x2, gate, nw)


def kernel(x, c, positions, w_mod, b_mod, mix_norm_pre, mix_norm_post, w_in, attn_sinks, w_gk_up, b_gk, gla_norm, w_branch_attn, w_branch_gla, w_out, ffn_norm_pre, ffn_norm_post, w_up, conv_w, conv_b, w_down):
    batch, seq, d = x.shape
    t = batch * seq
    depth = w_mod.shape[0]
    x2 = x.reshape(t, d)

    half = HEAD_DIM_A // 2
    inv_freq = ROPE_THETA ** (-jnp.arange(0, HEAD_DIM_A, 2, dtype=F32) / HEAD_DIM_A)
    invf_row = jnp.tile(inv_freq, LANES // half).reshape(1, LANES)
    sgn_row = jnp.tile(jnp.concatenate([-jnp.ones((half,), F32), jnp.ones((half,), F32)]),
                       LANES // HEAD_DIM_A).reshape(1, LANES)
    cos_t, sin_t = _rope_tables(positions.reshape(t, 1), invf_row, sgn_row)

    c_pad = jnp.zeros((SUBLANES, d), F32).at[:batch].set(c)

    o_qb = Q_A + 2 * KV_A
    o_lr = o_qb + 2 * QK_B + V_B
    o_og = o_lr + GK_RANK

    for l in range(depth):
        mod = _modulation(c_pad, w_mod[l], b_mod[l].reshape(1, -1))[:batch]
        sh1, sc1, g1, sh2, sc2, g2 = [m.reshape(batch, 1, d) for m in jnp.split(mod, 6, axis=-1)]

        w_in_t = w_in[l].T
        w_gk_pad = jnp.zeros((GK_PAD, QK_B), BF16).at[:GK_RANK].set(w_gk_up[l].astype(BF16))
        w_up_tiles = _retile_w_up(w_up[l])

        h = _norm_mod(x2, mix_norm_pre[l].reshape(1, d), sc1, sh1, seq)
        qkv_a, gk_lr = _proj_a(h, w_in_t, o_lr, cos_t, sin_t)
        pb = _matmul(h, w_in_t, row0=o_qb, n=o_lr - o_qb, act=None, name="proj_b")
        og = _matmul(h, w_in_t, row0=o_og, n=V_B, act=None, name="proj_og")
        gates = _matmul(h, w_in_t, row0=o_og + V_B, n=2 * d, act="sigmoid", name="gates")
        o_a = _swa(qkv_a, attn_sinks[l], seq)
        o_b = _gla(pb, og, gk_lr, w_gk_pad, b_gk[l].reshape(1, -1), gla_norm[l].reshape(1, -1), batch, seq)
        merged = _merge(o_a, o_b, w_branch_attn[l].astype(BF16), w_branch_gla[l].astype(BF16), gates)
        x2, h = _out_proj(merged, w_out[l].astype(BF16), x2, g1, mix_norm_post[l].reshape(1, d),
                          ffn_norm_pre[l].reshape(1, d), sc2, sh2, seq)

        x2 = _ffn(h, w_up_tiles, conv_w[l], conv_b[l].reshape(1, -1), w_down[l].astype(BF16),
                  x2, g2, ffn_norm_post[l].reshape(1, d), seq)

    return x2.reshape(batch, seq, d)
```

```python
import functools

import jax
import jax.numpy as jnp
from jax import lax
from jax.experimental import pallas as pl
from jax.experimental.pallas import tpu as pltpu

F32 = jnp.float32
BF16 = jnp.bfloat16

D_MODEL = 2048
HEAD_DIM_A = 64
HQ_A = 16
HKV_A = 4
WINDOW = 128
ROPE_THETA = 10000.0
H_B = 4
DK_B = 256
DV_B = 512
GK_RANK = 16
GK_NORMALIZER = 16.0
CHUNK = 64
D_FF = 5632
CONV_WIDTH = 3
EPS = 1e-6

Q_A = HQ_A * HEAD_DIM_A
KV_A = HKV_A * HEAD_DIM_A
QK_B = H_B * DK_B
V_B = H_B * DV_B
QKV_A = Q_A + 2 * KV_A
ROPE_COLS = Q_A + KV_A

LANES = 128
SUBLANES = 8
GK_PAD = LANES
VMEM_LIMIT = 48 * 1024 * 1024
VMEM_LIMIT_FFN = 56 * 1024 * 1024

TM_ROPE = 2048
TM_PROJ_A = 512
TM_MM, TN_MM = 1024, 1024
LR_ROWS = 16
TM_MERGE, TN_MERGE = 1024, 512
TM_OUT = 512
TM_FFN, TN_FFN = 1024, 512
ROWS_GLA = 256
BLOCKS_SWA = 4
ROWS_EPILOGUE = 256
TN_MOD = 1024


def _params(*sem, vmem=VMEM_LIMIT):
    return pltpu.CompilerParams(dimension_semantics=sem, vmem_limit_bytes=vmem)


def _dot(a, b):
    return jnp.dot(a, b, preferred_element_type=F32)


def _dot_nt(a, b):
    return lax.dot_general(a, b, (((1,), (1,)), ((), ())), preferred_element_type=F32)


def _dot_tn(a, b):
    return lax.dot_general(a, b, (((0,), (0,)), ((), ())), preferred_element_type=F32)


def _mod_kernel(c_ref, w_ref, b_ref, o_ref):
    c = c_ref[...]
    c_act = (c * jax.nn.sigmoid(c)).astype(BF16)
    o_ref[...] = _dot(c_act, w_ref[...].astype(BF16)) + b_ref[...]


def _modulation(c_pad, w_mod, b_mod):
    rows, d = c_pad.shape
    n = w_mod.shape[1]
    return pl.pallas_call(
        _mod_kernel,
        out_shape=jax.ShapeDtypeStruct((rows, n), F32),
        grid=(n // TN_MOD,),
        in_specs=[
            pl.BlockSpec((rows, d), lambda j: (0, 0)),
            pl.BlockSpec((d, TN_MOD), lambda j: (0, j)),
            pl.BlockSpec((1, TN_MOD), lambda j: (0, j)),
        ],
        out_specs=pl.BlockSpec((rows, TN_MOD), lambda j: (0, j)),
        compiler_params=_params("parallel"),
        name="mod",
    )(c_pad, w_mod, b_mod)


def _retile_kernel(wg_ref, wv_ref, o_ref):
    tn = wg_ref.shape[1]
    o_ref[:, :tn] = wg_ref[...].astype(o_ref.dtype)
    o_ref[:, tn:] = wv_ref[...].astype(o_ref.dtype)


def _retile_w_up(w_up):
    d = w_up.shape[0]
    tn = TN_FFN
    nj = D_FF // tn
    return pl.pallas_call(
        _retile_kernel,
        out_shape=jax.ShapeDtypeStruct((nj, d, 2 * tn), BF16),
        grid=(nj,),
        in_specs=[
            pl.BlockSpec((d, tn), lambda j: (0, j)),
            pl.BlockSpec((d, tn), lambda j: (0, j + nj)),
        ],
        out_specs=pl.BlockSpec((None, d, 2 * tn), lambda j: (j, 0, 0)),
        compiler_params=_params("parallel"),
        name="retile_w_up",
    )(w_up, w_up)


def _rope_table_kernel(pos_ref, invf_ref, sgn_ref, cos_ref, sin_ref):
    ang = pos_ref[...].astype(F32) * invf_ref[...]
    cos_ref[...] = jnp.cos(ang)
    sin_ref[...] = jnp.sin(ang) * sgn_ref[...]


def _rope_tables(pos_col, invf_row, sgn_row):
    t = pos_col.shape[0]
    return pl.pallas_call(
        _rope_table_kernel,
        out_shape=(jax.ShapeDtypeStruct((t, LANES), F32), jax.ShapeDtypeStruct((t, LANES), F32)),
        grid=(t // TM_ROPE,),
        in_specs=[
            pl.BlockSpec((TM_ROPE, 1), lambda i: (i, 0)),
            pl.BlockSpec((1, LANES), lambda i: (0, 0)),
            pl.BlockSpec((1, LANES), lambda i: (0, 0)),
        ],
        out_specs=(
            pl.BlockSpec((TM_ROPE, LANES), lambda i: (i, 0)),
            pl.BlockSpec((TM_ROPE, LANES), lambda i: (i, 0)),
        ),
        compiler_params=_params("parallel"),
        name="rope_tables",
    )(pos_col, invf_row, sgn_row)


def _proj_a_kernel(x_ref, nw_ref, sc_ref, sh_ref, w_ref, wlr_ref, cos_ref, sin_ref, qkv_ref, lr_ref, h_ref,
                   wbf_ref):
    @pl.when(pl.program_id(0) == 0)
    def _():
        wbf_ref[:QKV_A, :] = w_ref[...].astype(BF16)
        wbf_ref[QKV_A:QKV_A + LR_ROWS, :] = wlr_ref[...].astype(BF16)
        wbf_ref[QKV_A + LR_ROWS:, :] = jnp.zeros((GK_PAD - LR_ROWS, wbf_ref.shape[1]), BF16)

    half = HEAD_DIM_A // 2
    for r0 in range(0, x_ref.shape[0], ROWS_EPILOGUE):
        rows = slice(r0, r0 + ROWS_EPILOGUE)
        x = x_ref[rows, :]
        ms = jnp.mean(x * x, axis=-1, keepdims=True)
        y = x * lax.rsqrt(ms + EPS) * nw_ref[...]
        h = (y * (1.0 + sc_ref[...]) + sh_ref[...]).astype(h_ref.dtype)
        h_ref[rows, :] = h
        acc = _dot_nt(h, wbf_ref[...])
        cos = cos_ref[rows, :]
        sin = sin_ref[rows, :]
        lane = lax.broadcasted_iota(jnp.int32, cos.shape, 1)
        first_half = (lane % HEAD_DIM_A) < half
        for c in range(ROPE_COLS // LANES):
            y = acc[:, c * LANES:(c + 1) * LANES]
            partner = jnp.where(first_half, pltpu.roll(y, LANES - half, 1), pltpu.roll(y, half, 1))
            r = y * cos + partner * sin
            if c < Q_A // LANES:
                r = r * (HEAD_DIM_A ** -0.5)
            qkv_ref[rows, c * LANES:(c + 1) * LANES] = r.astype(qkv_ref.dtype)
        qkv_ref[rows, ROPE_COLS:QKV_A] = acc[:, ROPE_COLS:QKV_A].astype(qkv_ref.dtype)
        lr_ref[rows, :] = acc[:, QKV_A:QKV_A + GK_PAD].astype(lr_ref.dtype)


def _proj_a(x2, nw, sc, sh, w_in_t, lr_row0, cos_t, sin_t, seq):
    t, d = x2.shape
    tm = TM_PROJ_A
    per_seq = seq // tm
    once = pl.Buffered(1)
    per_batch = pl.BlockSpec((None, 1, d), lambda i: (i // per_seq, 0, 0))
    return pl.pallas_call(
        _proj_a_kernel,
        out_shape=(jax.ShapeDtypeStruct((t, QKV_A), BF16), jax.ShapeDtypeStruct((t, GK_PAD), BF16),
                   jax.ShapeDtypeStruct((t, d), BF16)),
        grid=(t // tm,),
        in_specs=[
            pl.BlockSpec((tm, d), lambda i: (i, 0)),
            pl.BlockSpec((1, d), lambda i: (0, 0)),
            per_batch, per_batch,
            pl.BlockSpec((QKV_A, d), lambda i: (0, 0), pipeline_mode=once),
            pl.BlockSpec((pl.Element(LR_ROWS), pl.Element(d)), lambda i: (lr_row0, 0), pipeline_mode=once),
            pl.BlockSpec((tm, LANES), lambda i: (i, 0)),
            pl.BlockSpec((tm, LANES), lambda i: (i, 0)),
        ],
        out_specs=(
            pl.BlockSpec((tm, QKV_A), lambda i: (i, 0)),
            pl.BlockSpec((tm, GK_PAD), lambda i: (i, 0)),
            pl.BlockSpec((tm, d), lambda i: (i, 0)),
        ),
        scratch_shapes=[pltpu.VMEM((QKV_A + GK_PAD, d), BF16)],
        compiler_params=_params("arbitrary"),
        name="proj_a",
    )(x2, nw, sc, sh, w_in_t, w_in_t, cos_t, sin_t)


def _mm_kernel(a_ref, wt_ref, o_ref, wbf_ref, *, act):
    @pl.when(pl.program_id(1) == 0)
    def _():
        wbf_ref[...] = wt_ref[...].astype(BF16)

    acc = _dot_nt(a_ref[...], wbf_ref[...])
    if act == "sigmoid":
        acc = jax.nn.sigmoid(acc)
    o_ref[...] = acc.astype(o_ref.dtype)


def _matmul(a, w_t, *, row0, n, act, name):
    m, k = a.shape
    tm, tn = TM_MM, TN_MM
    assert row0 % SUBLANES == 0 and n % tn == 0
    return pl.pallas_call(
        functools.partial(_mm_kernel, act=act),
        out_shape=jax.ShapeDtypeStruct((m, n), BF16),
        grid=(n // tn, m // tm),
        in_specs=[
            pl.BlockSpec((tm, k), lambda j, i: (i, 0)),
            pl.BlockSpec((pl.Element(tn), pl.Element(k)),
                         lambda j, i: (pl.multiple_of(row0 + j * tn, SUBLANES), 0)),
        ],
        out_specs=pl.BlockSpec((tm, tn), lambda j, i: (i, j)),
        scratch_shapes=[pltpu.VMEM((tn, k), BF16)],
        compiler_params=_params("parallel", "arbitrary"),
        name=name,
    )(a, w_t)


def _swa_kernel(sink_ref, q_ref, kc_ref, kp_ref, vc_ref, vp_ref, o_ref, *, blocks_per_seq):
    w = WINDOW
    for blk in range(q_ref.shape[0] // w):
        rows = slice(blk * w, (blk + 1) * w)
        if blk == 0:
            first = (pl.program_id(0) * (q_ref.shape[0] // w)) % blocks_per_seq == 0
            prev_bias = jnp.where(first, -jnp.inf, 0.0)
            kp, vp = kp_ref, vp_ref
        else:
            prev_bias = 0.0
            prev_rows = slice((blk - 1) * w, blk * w)
            kp, vp = kc_ref.at[prev_rows, :], vc_ref.at[prev_rows, :]
        _swa_block(sink_ref, q_ref.at[rows, :], kc_ref.at[rows, :], kp, vc_ref.at[rows, :], vp,
                   o_ref.at[rows, :], prev_bias)


def _swa_block(sink_ref, q_ref, kc_ref, kp_ref, vc_ref, vp_ref, o_ref, prev_bias):
    w = WINDOW
    dh = HEAD_DIM_A
    slot = lax.broadcasted_iota(jnp.int32, (w, w), 0)
    qry = lax.broadcasted_iota(jnp.int32, (w, w), 1)
    from_prev = slot > qry
    lo = lax.broadcasted_iota(jnp.int32, (2 * w, LANES), 1) < dh
    zero_k = jnp.zeros((2 * w, LANES), BF16)
    zero_v = jnp.zeros((dh, 2 * w), BF16)

    v_bdts = []
    windows = []
    for pair in range(HKV_A // 2):
        ls = slice(pair * LANES, (pair + 1) * LANES)
        kt = jnp.concatenate([kp_ref[:, ls], kc_ref[:, ls]], axis=0)
        kt_r = pltpu.roll(kt.astype(F32), dh, 1).astype(BF16)
        vt_t = jnp.concatenate([vp_ref[:, ls], vc_ref[:, ls]], axis=0).astype(F32).T.astype(BF16)
        for e in range(2):
            hkv = 2 * pair + e
            k_lo, k_hi = (kt, kt_r) if e == 0 else (kt_r, kt)
            k_bd = jnp.concatenate([jnp.where(lo, k_lo, zero_k), jnp.where(lo, zero_k, k_hi)], axis=0)
            v_t = vt_t[e * dh:(e + 1) * dh, :]
            v_bdts.append(jnp.concatenate([jnp.concatenate([v_t, zero_v], axis=1),
                                           jnp.concatenate([zero_v, v_t], axis=1)], axis=0))
            q2 = jnp.concatenate([q_ref[:, (2 * hkv + t) * LANES:(2 * hkv + t + 1) * LANES] for t in range(2)],
                                 axis=0)
            s_t = _dot_nt(k_bd, q2)
            for t in range(2):
                for hh in range(2):
                    blk = s_t[2 * w * hh:2 * w * (hh + 1), t * w:(t + 1) * w]
                    windows.append(jnp.where(from_prev, blk[:w, :] + prev_bias, blk[w:, :]))
    s_all = jnp.concatenate(windows, axis=1)
    sink = jnp.concatenate([jnp.full((1, w), sink_ref[n], F32) for n in range(HQ_A)], axis=1)
    m = jnp.maximum(jnp.max(s_all, axis=0, keepdims=True), sink)
    p = jnp.exp(s_all - m)
    denom = jnp.sum(p, axis=0, keepdims=True) + jnp.exp(sink - m)
    p = p * (1.0 / denom)

    zero_p = jnp.zeros((w, w), F32)
    for hkv in range(HKV_A):
        cols = []
        for t in range(2):
            parts = []
            for hh in range(2):
                n = 4 * hkv + 2 * t + hh
                ph = p[:, n * w:(n + 1) * w]
                parts += [jnp.where(from_prev, ph, zero_p), jnp.where(from_prev, zero_p, ph)]
            cols.append(jnp.concatenate(parts, axis=0))
        probs_t = jnp.concatenate(cols, axis=1).astype(BF16)
        o_t = _dot(v_bdts[hkv], probs_t)
        for t in range(2):
            tile = 2 * hkv + t
            o_ref[:, tile * LANES:(tile + 1) * LANES] = o_t[:, t * w:(t + 1) * w].T.astype(o_ref.dtype)


def _swa(qkv, sinks, seq):
    t = qkv.shape[0]
    w = WINDOW
    nb = seq // w
    q_blocks = Q_A // KV_A

    def prev(i):
        return jnp.maximum(i - 1, 0)

    nblk = BLOCKS_SWA
    rows = nblk * w
    return pl.pallas_call(
        functools.partial(_swa_kernel, blocks_per_seq=nb),
        out_shape=jax.ShapeDtypeStruct((t, Q_A), BF16),
        grid=(t // rows,),
        in_specs=[
            pl.BlockSpec(memory_space=pltpu.SMEM),
            pl.BlockSpec((rows, Q_A), lambda i: (i, 0)),
            pl.BlockSpec((rows, KV_A), lambda i: (i, q_blocks)),
            pl.BlockSpec((w, KV_A), lambda i: (prev(i * nblk), q_blocks)),
            pl.BlockSpec((rows, KV_A), lambda i: (i, q_blocks + 1)),
            pl.BlockSpec((w, KV_A), lambda i: (prev(i * nblk), q_blocks + 1)),
        ],
        out_specs=pl.BlockSpec((rows, Q_A), lambda i: (i, 0)),
        compiler_params=_params("parallel"),
        name="swa",
    )(sinks, qkv, qkv, qkv, qkv, qkv)


def _gla_kernel(q_ref, k_ref, v_ref, og_ref, lr_ref, wup_ref, bgk_ref, gn_ref, o_ref, st_ref):
    @pl.when(pl.program_id(1) == 0)
    def _():
        st_ref[...] = jnp.zeros_like(st_ref)

    c = CHUNK
    rows_blk = q_ref.shape[0]
    nch = rows_blk // c
    mid = c // 2
    z = _dot(lr_ref[...], wup_ref[...]) + bgk_ref[...]
    g = jax.nn.log_sigmoid(z) / GK_NORMALIZER
    g1 = g.astype(BF16)
    rem = g - g1.astype(F32)
    g2 = rem.astype(BF16)
    g3 = (rem - g2.astype(F32)).astype(BF16)

    def sums(mat):
        return _dot(mat, g1) + _dot(mat, g2) + _dot(mat, g3)

    shift = c.bit_length() - 1
    r_i = lax.broadcasted_iota(jnp.int32, (rows_blk, rows_blk), 0)
    c_i = lax.broadcasted_iota(jnp.int32, (rows_blk, rows_blk), 1)
    same = (r_i >> shift) == (c_i >> shift)
    r_in = r_i & (c - 1)
    c_in = c_i & (c - 1)
    plus = same & (c_in <= r_in) & (c_in > mid)
    minus = same & (c_in > r_in) & (c_in <= mid)
    d_mid = jnp.where(plus, 1.0, 0.0) - jnp.where(minus, 1.0, 0.0)
    b_rel = sums(d_mid.astype(BF16))
    sel_rows = 2 * SUBLANES
    s_r = lax.broadcasted_iota(jnp.int32, (sel_rows, rows_blk), 0)
    s_c = lax.broadcasted_iota(jnp.int32, (sel_rows, rows_blk), 1)
    s_chunk = s_c >> shift
    pick = ((s_r == s_chunk) & ((s_c & (c - 1)) <= mid)) | (s_r == s_chunk + nch)
    b_sel = sums(jnp.where(pick, 1.0, 0.0).astype(BF16))

    q = q_ref[...].astype(F32) * (DK_B ** -0.5)
    k = k_ref[...].astype(F32)
    qs = q * jnp.exp(b_rel)
    ks = k * jnp.exp(-b_rel)
    qs_b = qs.astype(BF16)
    ks_b = ks.astype(BF16)
    gn = gn_ref[...]
    cr_i = lax.broadcasted_iota(jnp.int32, (c, c), 0)
    cc_i = lax.broadcasted_iota(jnp.int32, (c, c), 1)
    causal = cr_i >= cc_i
    for ch in range(nch):
        rows = slice(ch * c, (ch + 1) * c)
        b_mid = b_sel[ch:ch + 1, :]
        b_last = b_sel[nch + ch:nch + ch + 1, :]
        q_in = (qs[rows, :] * jnp.exp(b_mid)).astype(BF16)
        k_st = (ks[rows, :] * jnp.exp(b_last - b_mid)).astype(BF16)
        decay = jnp.exp(b_last)
        for h in range(H_B):
            sk = slice(h * DK_B, (h + 1) * DK_B)
            sv = slice(h * DV_B, (h + 1) * DV_B)
            v = v_ref[rows, sv]
            a = jnp.where(causal, _dot_nt(qs_b[rows, sk], ks_b[rows, sk]), 0.0)
            st = st_ref[h]
            o = _dot(a.astype(BF16), v) + _dot_nt(q_in[:, sk], st.astype(BF16))
            st_ref[h] = st * decay[:, sk] + _dot_tn(v, k_st[:, sk])
            ms = jnp.mean(o * o, axis=-1, keepdims=True)
            og = og_ref[rows, sv].astype(F32)
            out = (o * lax.rsqrt(ms + EPS) * gn) * (og * jax.nn.sigmoid(og))
            o_ref[rows, sv] = out.astype(o_ref.dtype)


def _gla(pb, og, lr, w_up_pad, b_gk, gla_norm, batch, seq):
    t = pb.shape[0]
    c = ROWS_GLA
    nc = seq // c

    def rows(b, s):
        return b * nc + s

    return pl.pallas_call(
        _gla_kernel,
        out_shape=jax.ShapeDtypeStruct((t, V_B), BF16),
        grid=(batch, nc),
        in_specs=[
            pl.BlockSpec((c, QK_B), lambda b, s: (rows(b, s), 0)),
            pl.BlockSpec((c, QK_B), lambda b, s: (rows(b, s), 1)),
            pl.BlockSpec((c, V_B), lambda b, s: (rows(b, s), 1)),
            pl.BlockSpec((c, V_B), lambda b, s: (rows(b, s), 0)),
            pl.BlockSpec((c, GK_PAD), lambda b, s: (rows(b, s), 0)),
            pl.BlockSpec((GK_PAD, QK_B), lambda b, s: (0, 0)),
            pl.BlockSpec((1, QK_B), lambda b, s: (0, 0)),
            pl.BlockSpec((1, DV_B), lambda b, s: (0, 0)),
        ],
        out_specs=pl.BlockSpec((c, V_B), lambda b, s: (rows(b, s), 0)),
        scratch_shapes=[pltpu.VMEM((H_B, DV_B, DK_B), F32)],
        compiler_params=_params("parallel", "arbitrary"),
        name="gla",
    )(pb, pb, pb, og, lr, w_up_pad, b_gk, gla_norm)


def _merge_kernel(oa_ref, ob_ref, wa_ref, wb_ref, ga_ref, gb_ref, o_ref):
    ta = _dot(oa_ref[...], wa_ref[...])
    tb = _dot(ob_ref[...], wb_ref[...])
    out = ga_ref[...].astype(F32) * ta + gb_ref[...].astype(F32) * tb
    o_ref[...] = out.astype(o_ref.dtype)


def _merge(o_a, o_b, w_a, w_b, gates):
    t = o_a.shape[0]
    d = w_a.shape[1]
    tm, tn = TM_MERGE, TN_MERGE
    nj = d // tn
    return pl.pallas_call(
        _merge_kernel,
        out_shape=jax.ShapeDtypeStruct((t, d), BF16),
        grid=(t // tm, nj),
        in_specs=[
            pl.BlockSpec((tm, Q_A), lambda i, j: (i, 0)),
            pl.BlockSpec((tm, V_B), lambda i, j: (i, 0)),
            pl.BlockSpec((Q_A, tn), lambda i, j: (0, j)),
            pl.BlockSpec((V_B, tn), lambda i, j: (0, j)),
            pl.BlockSpec((tm, tn), lambda i, j: (i, j)),
            pl.BlockSpec((tm, tn), lambda i, j: (i, j + nj)),
        ],
        out_specs=pl.BlockSpec((tm, tn), lambda i, j: (i, j)),
        compiler_params=_params("parallel", "parallel"),
        name="merge",
    )(o_a, o_b, w_a, w_b, gates, gates)


def _rms_residual(x, gate, y, nw):
    ms = jnp.mean(y * y, axis=-1, keepdims=True)
    return x + gate * (y * lax.rsqrt(ms + EPS) * nw)


def _out_proj_kernel(a_ref, w_ref, x_ref, g_ref, nw_ref, nw2_ref, sc2_ref, sh2_ref, o_ref, h_ref):
    for r in range(0, a_ref.shape[0], ROWS_EPILOGUE):
        rows = slice(r, r + ROWS_EPILOGUE)
        y = _dot(a_ref[rows, :], w_ref[...])
        x1 = _rms_residual(x_ref[rows, :], g_ref[...], y, nw_ref[...])
        o_ref[rows, :] = x1
        ms = jnp.mean(x1 * x1, axis=-1, keepdims=True)
        h2 = x1 * lax.rsqrt(ms + EPS) * nw2_ref[...]
        h_ref[rows, :] = (h2 * (1.0 + sc2_ref[...]) + sh2_ref[...]).astype(h_ref.dtype)


def _out_proj(a, w, x2, gate, nw, nw2, sc2, sh2, seq):
    t, k = a.shape
    d = w.shape[1]
    tm = TM_OUT
    per_seq = seq // tm
    row = pl.BlockSpec((1, d), lambda i: (0, 0))
    per_batch = pl.BlockSpec((None, 1, d), lambda i: (i // per_seq, 0, 0))
    return pl.pallas_call(
        _out_proj_kernel,
        out_shape=(jax.ShapeDtypeStruct((t, d), F32), jax.ShapeDtypeStruct((t, d), BF16)),
        grid=(t // tm,),
        in_specs=[
            pl.BlockSpec((tm, k), lambda i: (i, 0)),
            pl.BlockSpec((k, d), lambda i: (0, 0)),
            pl.BlockSpec((tm, d), lambda i: (i, 0)),
            per_batch, row, row, per_batch, per_batch,
        ],
        out_specs=(pl.BlockSpec((tm, d), lambda i: (i, 0)), pl.BlockSpec((tm, d), lambda i: (i, 0))),
        compiler_params=_params("parallel"),
        name="out_proj",
    )(a, w, x2, gate, nw, nw2, sc2, sh2)


def _ffn_kernel(h_ref, wu_ref, cwg_ref, cwv_ref, cbg_ref, cbv_ref, wd_ref, x_hbm, g_ref, nw_ref,
                o_ref, x_ref, x_sem, act0_ref, act1_ref, carry_ref, *, nj, tiles_per_seq):
    i = pl.program_id(0)
    j = pl.program_id(1)
    tm = h_ref.shape[0]
    tn = wu_ref.shape[1] // 2
    seq_start = (i % tiles_per_seq) == 0

    def x_copy():
        return pltpu.make_async_copy(x_hbm.at[pl.ds(pl.multiple_of(i * tm, tm), tm), :], x_ref, x_sem)

    def up(act_ref):
        h = h_ref[...]
        jc = jnp.minimum(j, nj - 1)
        tails = []

        def conv(u, cols, cw_ref, cb_ref):
            row = lax.broadcasted_iota(jnp.int32, u.shape, 0)
            c6 = jnp.where(seq_start, 0.0, carry_ref[jc, SUBLANES - 2:SUBLANES - 1, cols])
            c7 = jnp.where(seq_start, 0.0, carry_ref[jc, SUBLANES - 1:SUBLANES, cols])
            u1 = jnp.where(row == 0, c7, pltpu.roll(u, 1, 0))
            u2 = jnp.where(row == 0, c6, jnp.where(row == 1, c7, pltpu.roll(u, 2, 0)))
            tails.append((cols, u[tm - SUBLANES:, :]))
            return cw_ref[0:1, :] * u2 + cw_ref[1:2, :] * u1 + cw_ref[2:3, :] * u + cb_ref[...]

        u = _dot(h, wu_ref[...])
        gate = conv(u[:, :tn], slice(0, tn), cwg_ref, cbg_ref)
        val = conv(u[:, tn:], slice(tn, 2 * tn), cwv_ref, cbv_ref)
        act_ref[...] = (gate * jax.nn.sigmoid(gate) * val).astype(act_ref.dtype)
        return jc, tails

    def save_carry(jc, tails):
        for cols, tail in tails:
            carry_ref[jc, :, cols] = tail

    def down(act_ref):
        o_ref[...] += _dot(act_ref[...], wd_ref[...])

    @pl.when(j == 0)
    def _():
        x_copy().start()
        o_ref[...] = jnp.zeros_like(o_ref)
        save_carry(*up(act0_ref))

    @pl.when((j > 0) & (j < nj) & (j % 2 == 1))
    def _():
        jc, tails = up(act1_ref)
        down(act0_ref)
        save_carry(jc, tails)

    @pl.when((j > 0) & (j < nj) & (j % 2 == 0))
    def _():
        jc, tails = up(act0_ref)
        down(act1_ref)
        save_carry(jc, tails)

    @pl.when(j == nj)
    def _():
        x_copy().wait()
        down(act0_ref if (nj - 1) % 2 == 0 else act1_ref)
        for r in range(0, tm, ROWS_EPILOGUE):
            rows = slice(r, r + ROWS_EPILOGUE)
            o_ref[rows, :] = _rms_residual(x_ref[rows, :], g_ref[...], o_ref[rows, :], nw_ref[...])


def _ffn(h, w_up, conv_w, conv_b, w_down, x2, gate, nw, seq):
    t, d = h.shape
    tm, tn = TM_FFN, TN_FFN
    nj = D_FF // tn
    tiles_per_seq = seq // tm

    def up_tile(j):
        return jnp.minimum(j, nj - 1)

    def down_tile(j):
        return jnp.maximum(j - 1, 0)

    return pl.pallas_call(
        functools.partial(_ffn_kernel, nj=nj, tiles_per_seq=tiles_per_seq),
        out_shape=jax.ShapeDtypeStruct((t, d), F32),
        grid=(t // tm, nj + 1),
        in_specs=[
            pl.BlockSpec((tm, d), lambda i, j: (i, 0)),
            pl.BlockSpec((None, d, 2 * tn), lambda i, j: (up_tile(j), 0, 0)),
            pl.BlockSpec((CONV_WIDTH, tn), lambda i, j: (0, up_tile(j))),
            pl.BlockSpec((CONV_WIDTH, tn), lambda i, j: (0, up_tile(j) + nj)),
            pl.BlockSpec((1, tn), lambda i, j: (0, up_tile(j))),
            pl.BlockSpec((1, tn), lambda i, j: (0, up_tile(j) + nj)),
            pl.BlockSpec((tn, d), lambda i, j: (down_tile(j), 0)),
            pl.BlockSpec(memory_space=pl.ANY),
            pl.BlockSpec((None, 1, d), lambda i, j: (i // tiles_per_seq, 0, 0)),
            pl.BlockSpec((1, d), lambda i, j: (0, 0)),
        ],
        out_specs=pl.BlockSpec((tm, d), lambda i, j: (i, 0)),
        scratch_shapes=[
            pltpu.VMEM((tm, d), F32),
            pltpu.SemaphoreType.DMA,
            pltpu.VMEM((tm, tn), BF16),
            pltpu.VMEM((tm, tn), BF16),
            pltpu.VMEM((nj, SUBLANES, 2 * tn), F32),
        ],
        compiler_params=_params("arbitrary", "arbitrary", vmem=VMEM_LIMIT_FFN),
        name="ffn",
    )(h, w_up, conv_w, conv_w, conv_b, conv_b, w_down, x2, gate, nw)


def kernel(x, c, positions, w_mod, b_mod, mix_norm_pre, mix_norm_post, w_in, attn_sinks, w_gk_up, b_gk, gla_norm, w_branch_attn, w_branch_gla, w_out, ffn_norm_pre, ffn_norm_post, w_up, conv_w, conv_b, w_down):
    batch, seq, d = x.shape
    t = batch * seq
    depth = w_mod.shape[0]
    x2 = x.reshape(t, d)

    half = HEAD_DIM_A // 2
    inv_freq = ROPE_THETA ** (-jnp.arange(0, HEAD_DIM_A, 2, dtype=F32) / HEAD_DIM_A)
    invf_row = jnp.tile(inv_freq, LANES // half).reshape(1, LANES)
    sgn_row = jnp.tile(jnp.concatenate([-jnp.ones((half,), F32), jnp.ones((half,), F32)]),
                       LANES // HEAD_DIM_A).reshape(1, LANES)
    cos_t, sin_t = _rope_tables(positions.reshape(t, 1), invf_row, sgn_row)

    c_pad = jnp.zeros((SUBLANES, d), F32).at[:batch].set(c)

    o_qb = Q_A + 2 * KV_A
    o_lr = o_qb + 2 * QK_B + V_B
    o_og = o_lr + GK_RANK

    for l in range(depth):
        mod = _modulation(c_pad, w_mod[l], b_mod[l].reshape(1, -1))[:batch]
        sh1, sc1, g1, sh2, sc2, g2 = [m.reshape(batch, 1, d) for m in jnp.split(mod, 6, axis=-1)]

        w_in_t = w_in[l].T
        w_gk_pad = jnp.zeros((GK_PAD, QK_B), BF16).at[:GK_RANK].set(w_gk_up[l].astype(BF16))
        w_up_tiles = _retile_w_up(w_up[l])

        qkv_a, gk_lr, h = _proj_a(x2, mix_norm_pre[l].reshape(1, d), sc1, sh1, w_in_t, o_lr, cos_t, sin_t, seq)
        pb = _matmul(h, w_in_t, row0=o_qb, n=o_lr - o_qb, act=None, name="proj_b")
        og = _matmul(h, w_in_t, row0=o_og, n=V_B, act=None, name="proj_og")
        gates = _matmul(h, w_in_t, row0=o_og + V_B, n=2 * d, act="sigmoid", name="gates")
        o_a = _swa(qkv_a, attn_sinks[l], seq)
        o_b = _gla(pb, og, gk_lr, w_gk_pad, b_gk[l].reshape(1, -1), gla_norm[l].reshape(1, -1), batch, seq)
        merged = _merge(o_a, o_b, w_branch_attn[l].astype(BF16), w_branch_gla[l].astype(BF16), gates)
        x2, h = _out_proj(merged, w_out[l].astype(BF16), x2, g1, mix_norm_post[l].reshape(1, d),
                          ffn_norm_pre[l].reshape(1, d), sc2, sh2, seq)

        x2 = _ffn(h, w_up_tiles, conv_w[l], conv_b[l].reshape(1, -1), w_down[l].astype(BF16),
                  x2, g2, ffn_norm_post[l].reshape(1, d), seq)

    return x2.reshape(batch, seq, d)
```

```python
import functools

import jax
import jax.numpy as jnp
from jax import lax
from jax.experimental import pallas as pl
from jax.experimental.pallas import tpu as pltpu

F32 = jnp.float32
BF16 = jnp.bfloat16

D_MODEL = 2048
HEAD_DIM_A = 64
HQ_A = 16
HKV_A = 4
WINDOW = 128
ROPE_THETA = 10000.0
H_B = 4
DK_B = 256
DV_B = 512
GK_RANK = 16
GK_NORMALIZER = 16.0
CHUNK = 64
D_FF = 5632
CONV_WIDTH = 3
EPS = 1e-6

Q_A = HQ_A * HEAD_DIM_A
KV_A = HKV_A * HEAD_DIM_A
QK_B = H_B * DK_B
V_B = H_B * DV_B
QKV_A = Q_A + 2 * KV_A
ROPE_COLS = Q_A + KV_A

LANES = 128
SUBLANES = 8
GK_PAD = LANES
VMEM_LIMIT = 48 * 1024 * 1024
VMEM_LIMIT_FFN = 56 * 1024 * 1024

TM_ROPE = 2048
TM_PROJ_A = 512
TM_MM, TN_MM = 1024, 1024
LR_ROWS = 16
TM_MERGE, TN_MERGE = 1024, 512
TM_OUT = 512
TM_FFN, TN_FFN = 1024, 512
ROWS_GLA = 256
BLOCKS_SWA = 4
ROWS_EPILOGUE = 256
TN_MOD = 1024


def _params(*sem, vmem=VMEM_LIMIT):
    return pltpu.CompilerParams(dimension_semantics=sem, vmem_limit_bytes=vmem)


def _dot(a, b):
    return jnp.dot(a, b, preferred_element_type=F32)


def _dot_nt(a, b):
    return lax.dot_general(a, b, (((1,), (1,)), ((), ())), preferred_element_type=F32)


def _dot_tn(a, b):
    return lax.dot_general(a, b, (((0,), (0,)), ((), ())), preferred_element_type=F32)


def _mod_kernel(c_ref, w_ref, b_ref, o_ref):
    c = c_ref[...]
    c_act = (c * jax.nn.sigmoid(c)).astype(BF16)
    o_ref[...] = _dot(c_act, w_ref[...].astype(BF16)) + b_ref[...]


def _modulation(c_pad, w_mod, b_mod):
    rows, d = c_pad.shape
    n = w_mod.shape[1]
    return pl.pallas_call(
        _mod_kernel,
        out_shape=jax.ShapeDtypeStruct((rows, n), F32),
        grid=(n // TN_MOD,),
        in_specs=[
            pl.BlockSpec((rows, d), lambda j: (0, 0)),
            pl.BlockSpec((d, TN_MOD), lambda j: (0, j)),
            pl.BlockSpec((1, TN_MOD), lambda j: (0, j)),
        ],
        out_specs=pl.BlockSpec((rows, TN_MOD), lambda j: (0, j)),
        compiler_params=_params("parallel"),
        name="mod",
    )(c_pad, w_mod, b_mod)


def _retile_kernel(wg_ref, wv_ref, o_ref):
    tn = wg_ref.shape[1]
    o_ref[:, :tn] = wg_ref[...].astype(o_ref.dtype)
    o_ref[:, tn:] = wv_ref[...].astype(o_ref.dtype)


def _retile_w_up(w_up):
    d = w_up.shape[0]
    tn = TN_FFN
    nj = D_FF // tn
    return pl.pallas_call(
        _retile_kernel,
        out_shape=jax.ShapeDtypeStruct((nj, d, 2 * tn), BF16),
        grid=(nj,),
        in_specs=[
            pl.BlockSpec((d, tn), lambda j: (0, j)),
            pl.BlockSpec((d, tn), lambda j: (0, j + nj)),
        ],
        out_specs=pl.BlockSpec((None, d, 2 * tn), lambda j: (j, 0, 0)),
        compiler_params=_params("parallel"),
        name="retile_w_up",
    )(w_up, w_up)


def _rope_table_kernel(pos_ref, invf_ref, sgn_ref, cos_ref, sin_ref):
    ang = pos_ref[...].astype(F32) * invf_ref[...]
    cos_ref[...] = jnp.cos(ang)
    sin_ref[...] = jnp.sin(ang) * sgn_ref[...]


def _rope_tables(pos_col, invf_row, sgn_row):
    t = pos_col.shape[0]
    return pl.pallas_call(
        _rope_table_kernel,
        out_shape=(jax.ShapeDtypeStruct((t, LANES), F32), jax.ShapeDtypeStruct((t, LANES), F32)),
        grid=(t // TM_ROPE,),
        in_specs=[
            pl.BlockSpec((TM_ROPE, 1), lambda i: (i, 0)),
            pl.BlockSpec((1, LANES), lambda i: (0, 0)),
            pl.BlockSpec((1, LANES), lambda i: (0, 0)),
        ],
        out_specs=(
            pl.BlockSpec((TM_ROPE, LANES), lambda i: (i, 0)),
            pl.BlockSpec((TM_ROPE, LANES), lambda i: (i, 0)),
        ),
        compiler_params=_params("parallel"),
        name="rope_tables",
    )(pos_col, invf_row, sgn_row)


def _proj_a_kernel(x_ref, nw_ref, sc_ref, sh_ref, w_ref, wlr_ref, cos_ref, sin_ref, qkv_ref, lr_ref, h_ref,
                   wbf_ref):
    @pl.when(pl.program_id(0) == 0)
    def _():
        wbf_ref[:QKV_A, :] = w_ref[...].astype(BF16)
        wbf_ref[QKV_A:QKV_A + LR_ROWS, :] = wlr_ref[...].astype(BF16)
        wbf_ref[QKV_A + LR_ROWS:, :] = jnp.zeros((GK_PAD - LR_ROWS, wbf_ref.shape[1]), BF16)

    half = HEAD_DIM_A // 2
    for r0 in range(0, x_ref.shape[0], ROWS_EPILOGUE):
        rows = slice(r0, r0 + ROWS_EPILOGUE)
        x = x_ref[rows, :]
        ms = jnp.mean(x * x, axis=-1, keepdims=True)
        y = x * lax.rsqrt(ms + EPS) * nw_ref[...]
        h = (y * (1.0 + sc_ref[...]) + sh_ref[...]).astype(h_ref.dtype)
        h_ref[rows, :] = h
        acc = _dot_nt(h, wbf_ref[...])
        cos = cos_ref[rows, :]
        sin = sin_ref[rows, :]
        lane = lax.broadcasted_iota(jnp.int32, cos.shape, 1)
        first_half = (lane % HEAD_DIM_A) < half
        for c in range(ROPE_COLS // LANES):
            y = acc[:, c * LANES:(c + 1) * LANES]
            partner = jnp.where(first_half, pltpu.roll(y, LANES - half, 1), pltpu.roll(y, half, 1))
            r = y * cos + partner * sin
            if c < Q_A // LANES:
                r = r * (HEAD_DIM_A ** -0.5)
            qkv_ref[rows, c * LANES:(c + 1) * LANES] = r.astype(qkv_ref.dtype)
        qkv_ref[rows, ROPE_COLS:QKV_A] = acc[:, ROPE_COLS:QKV_A].astype(qkv_ref.dtype)
        lr_ref[rows, :] = acc[:, QKV_A:QKV_A + GK_PAD].astype(lr_ref.dtype)


def _proj_a(x2, nw, sc, sh, w_in_t, lr_row0, cos_t, sin_t, seq):
    t, d = x2.shape
    tm = TM_PROJ_A
    per_seq = seq // tm
    once = pl.Buffered(1)
    per_batch = pl.BlockSpec((None, 1, d), lambda i: (i // per_seq, 0, 0))
    return pl.pallas_call(
        _proj_a_kernel,
        out_shape=(jax.ShapeDtypeStruct((t, QKV_A), BF16), jax.ShapeDtypeStruct((t, GK_PAD), BF16),
                   jax.ShapeDtypeStruct((t, d), BF16)),
        grid=(t // tm,),
        in_specs=[
            pl.BlockSpec((tm, d), lambda i: (i, 0)),
            pl.BlockSpec((1, d), lambda i: (0, 0)),
            per_batch, per_batch,
            pl.BlockSpec((QKV_A, d), lambda i: (0, 0), pipeline_mode=once),
            pl.BlockSpec((pl.Element(LR_ROWS), pl.Element(d)), lambda i: (lr_row0, 0), pipeline_mode=once),
            pl.BlockSpec((tm, LANES), lambda i: (i, 0)),
            pl.BlockSpec((tm, LANES), lambda i: (i, 0)),
        ],
        out_specs=(
            pl.BlockSpec((tm, QKV_A), lambda i: (i, 0)),
            pl.BlockSpec((tm, GK_PAD), lambda i: (i, 0)),
            pl.BlockSpec((tm, d), lambda i: (i, 0)),
        ),
        scratch_shapes=[pltpu.VMEM((QKV_A + GK_PAD, d), BF16)],
        compiler_params=_params("arbitrary"),
        name="proj_a",
    )(x2, nw, sc, sh, w_in_t, w_in_t, cos_t, sin_t)


def _mm_kernel(a_ref, wt_ref, o_ref, wbf_ref):
    @pl.when(pl.program_id(1) == 0)
    def _():
        wbf_ref[...] = wt_ref[...].astype(BF16)

    o_ref[...] = _dot_nt(a_ref[...], wbf_ref[...]).astype(o_ref.dtype)


def _matmul(a, w_t, *, row0, n, skip_at, skip, name):
    m, k = a.shape
    tm, tn = TM_MM, TN_MM
    assert row0 % SUBLANES == 0 and skip % SUBLANES == 0 and n % tn == 0 and skip_at % tn == 0

    def w_row(j):
        return pl.multiple_of(row0 + j * tn + jnp.where(j * tn >= skip_at, skip, 0), SUBLANES)

    return pl.pallas_call(
        _mm_kernel,
        out_shape=jax.ShapeDtypeStruct((m, n), BF16),
        grid=(n // tn, m // tm),
        in_specs=[
            pl.BlockSpec((tm, k), lambda j, i: (i, 0)),
            pl.BlockSpec((pl.Element(tn), pl.Element(k)), lambda j, i: (w_row(j), 0)),
        ],
        out_specs=pl.BlockSpec((tm, tn), lambda j, i: (i, j)),
        scratch_shapes=[pltpu.VMEM((tn, k), BF16)],
        compiler_params=_params("parallel", "arbitrary"),
        name=name,
    )(a, w_t)


def _swa_kernel(sink_ref, q_ref, kc_ref, kp_ref, vc_ref, vp_ref, o_ref, *, blocks_per_seq):
    w = WINDOW
    for blk in range(q_ref.shape[0] // w):
        rows = slice(blk * w, (blk + 1) * w)
        if blk == 0:
            first = (pl.program_id(0) * (q_ref.shape[0] // w)) % blocks_per_seq == 0
            prev_bias = jnp.where(first, -jnp.inf, 0.0)
            kp, vp = kp_ref, vp_ref
        else:
            prev_bias = 0.0
            prev_rows = slice((blk - 1) * w, blk * w)
            kp, vp = kc_ref.at[prev_rows, :], vc_ref.at[prev_rows, :]
        _swa_block(sink_ref, q_ref.at[rows, :], kc_ref.at[rows, :], kp, vc_ref.at[rows, :], vp,
                   o_ref.at[rows, :], prev_bias)


def _swa_block(sink_ref, q_ref, kc_ref, kp_ref, vc_ref, vp_ref, o_ref, prev_bias):
    w = WINDOW
    dh = HEAD_DIM_A
    slot = lax.broadcasted_iota(jnp.int32, (w, w), 0)
    qry = lax.broadcasted_iota(jnp.int32, (w, w), 1)
    from_prev = slot > qry
    lo = lax.broadcasted_iota(jnp.int32, (2 * w, LANES), 1) < dh
    zero_k = jnp.zeros((2 * w, LANES), BF16)
    zero_v = jnp.zeros((dh, 2 * w), BF16)

    v_bdts = []
    windows = []
    for pair in range(HKV_A // 2):
        ls = slice(pair * LANES, (pair + 1) * LANES)
        kt = jnp.concatenate([kp_ref[:, ls], kc_ref[:, ls]], axis=0)
        kt_r = pltpu.roll(kt.astype(F32), dh, 1).astype(BF16)
        vt_t = jnp.concatenate([vp_ref[:, ls], vc_ref[:, ls]], axis=0).astype(F32).T.astype(BF16)
        for e in range(2):
            hkv = 2 * pair + e
            k_lo, k_hi = (kt, kt_r) if e == 0 else (kt_r, kt)
            k_bd = jnp.concatenate([jnp.where(lo, k_lo, zero_k), jnp.where(lo, zero_k, k_hi)], axis=0)
            v_t = vt_t[e * dh:(e + 1) * dh, :]
            v_bdts.append(jnp.concatenate([jnp.concatenate([v_t, zero_v], axis=1),
                                           jnp.concatenate([zero_v, v_t], axis=1)], axis=0))
            q2 = jnp.concatenate([q_ref[:, (2 * hkv + t) * LANES:(2 * hkv + t + 1) * LANES] for t in range(2)],
                                 axis=0)
            s_t = _dot_nt(k_bd, q2)
            for t in range(2):
                for hh in range(2):
                    blk = s_t[2 * w * hh:2 * w * (hh + 1), t * w:(t + 1) * w]
                    windows.append(jnp.where(from_prev, blk[:w, :] + prev_bias, blk[w:, :]))
    s_all = jnp.concatenate(windows, axis=1)
    sink = jnp.concatenate([jnp.full((1, w), sink_ref[n], F32) for n in range(HQ_A)], axis=1)
    m = jnp.maximum(jnp.max(s_all, axis=0, keepdims=True), sink)
    p = jnp.exp(s_all - m)
    denom = jnp.sum(p, axis=0, keepdims=True) + jnp.exp(sink - m)
    p = p * (1.0 / denom)

    zero_p = jnp.zeros((w, w), F32)
    for hkv in range(HKV_A):
        cols = []
        for t in range(2):
            parts = []
            for hh in range(2):
                n = 4 * hkv + 2 * t + hh
                ph = p[:, n * w:(n + 1) * w]
                parts += [jnp.where(from_prev, ph, zero_p), jnp.where(from_prev, zero_p, ph)]
            cols.append(jnp.concatenate(parts, axis=0))
        probs_t = jnp.concatenate(cols, axis=1).astype(BF16)
        o_t = _dot(v_bdts[hkv], probs_t)
        for t in range(2):
            tile = 2 * hkv + t
            o_ref[:, tile * LANES:(tile + 1) * LANES] = o_t[:, t * w:(t + 1) * w].T.astype(o_ref.dtype)


def _swa(qkv, sinks, seq):
    t = qkv.shape[0]
    w = WINDOW
    nb = seq // w
    q_blocks = Q_A // KV_A

    def prev(i):
        return jnp.maximum(i - 1, 0)

    nblk = BLOCKS_SWA
    rows = nblk * w
    return pl.pallas_call(
        functools.partial(_swa_kernel, blocks_per_seq=nb),
        out_shape=jax.ShapeDtypeStruct((t, Q_A), BF16),
        grid=(t // rows,),
        in_specs=[
            pl.BlockSpec(memory_space=pltpu.SMEM),
            pl.BlockSpec((rows, Q_A), lambda i: (i, 0)),
            pl.BlockSpec((rows, KV_A), lambda i: (i, q_blocks)),
            pl.BlockSpec((w, KV_A), lambda i: (prev(i * nblk), q_blocks)),
            pl.BlockSpec((rows, KV_A), lambda i: (i, q_blocks + 1)),
            pl.BlockSpec((w, KV_A), lambda i: (prev(i * nblk), q_blocks + 1)),
        ],
        out_specs=pl.BlockSpec((rows, Q_A), lambda i: (i, 0)),
        compiler_params=_params("parallel"),
        name="swa",
    )(sinks, qkv, qkv, qkv, qkv, qkv)


def _gla_kernel(q_ref, k_ref, v_ref, og_ref, lr_ref, wup_ref, bgk_ref, gn_ref, o_ref, st_ref):
    @pl.when(pl.program_id(1) == 0)
    def _():
        st_ref[...] = jnp.zeros_like(st_ref)

    c = CHUNK
    rows_blk = q_ref.shape[0]
    nch = rows_blk // c
    mid = c // 2
    z = _dot(lr_ref[...], wup_ref[...]) + bgk_ref[...]
    g = jax.nn.log_sigmoid(z) / GK_NORMALIZER
    g1 = g.astype(BF16)
    rem = g - g1.astype(F32)
    g2 = rem.astype(BF16)
    g3 = (rem - g2.astype(F32)).astype(BF16)

    def sums(mat):
        return _dot(mat, g1) + _dot(mat, g2) + _dot(mat, g3)

    shift = c.bit_length() - 1
    r_i = lax.broadcasted_iota(jnp.int32, (rows_blk, rows_blk), 0)
    c_i = lax.broadcasted_iota(jnp.int32, (rows_blk, rows_blk), 1)
    same = (r_i >> shift) == (c_i >> shift)
    r_in = r_i & (c - 1)
    c_in = c_i & (c - 1)
    plus = same & (c_in <= r_in) & (c_in > mid)
    minus = same & (c_in > r_in) & (c_in <= mid)
    d_mid = jnp.where(plus, 1.0, 0.0) - jnp.where(minus, 1.0, 0.0)
    b_rel = sums(d_mid.astype(BF16))
    sel_rows = 2 * SUBLANES
    s_r = lax.broadcasted_iota(jnp.int32, (sel_rows, rows_blk), 0)
    s_c = lax.broadcasted_iota(jnp.int32, (sel_rows, rows_blk), 1)
    s_chunk = s_c >> shift
    pick = ((s_r == s_chunk) & ((s_c & (c - 1)) <= mid)) | (s_r == s_chunk + nch)
    b_sel = sums(jnp.where(pick, 1.0, 0.0).astype(BF16))

    q = q_ref[...].astype(F32) * (DK_B ** -0.5)
    k = k_ref[...].astype(F32)
    qs = q * jnp.exp(b_rel)
    ks = k * jnp.exp(-b_rel)
    qs_b = qs.astype(BF16)
    ks_b = ks.astype(BF16)
    gn = gn_ref[...]
    cr_i = lax.broadcasted_iota(jnp.int32, (c, c), 0)
    cc_i = lax.broadcasted_iota(jnp.int32, (c, c), 1)
    causal = cr_i >= cc_i
    for ch in range(nch):
        rows = slice(ch * c, (ch + 1) * c)
        b_mid = b_sel[ch:ch + 1, :]
        b_last = b_sel[nch + ch:nch + ch + 1, :]
        q_in = (qs[rows, :] * jnp.exp(b_mid)).astype(BF16)
        k_st = (ks[rows, :] * jnp.exp(b_last - b_mid)).astype(BF16)
        decay = jnp.exp(b_last)
        for h in range(H_B):
            sk = slice(h * DK_B, (h + 1) * DK_B)
            sv = slice(h * DV_B, (h + 1) * DV_B)
            v = v_ref[rows, sv]
            a = jnp.where(causal, _dot_nt(qs_b[rows, sk], ks_b[rows, sk]), 0.0)
            st = st_ref[h]
            o = _dot(a.astype(BF16), v) + _dot_nt(q_in[:, sk], st.astype(BF16))
            st_ref[h] = st * decay[:, sk] + _dot_tn(v, k_st[:, sk])
            ms = jnp.mean(o * o, axis=-1, keepdims=True)
            og = og_ref[rows, sv].astype(F32)
            out = (o * lax.rsqrt(ms + EPS) * gn) * (og * jax.nn.sigmoid(og))
            o_ref[rows, sv] = out.astype(o_ref.dtype)


def _gla(pb, lr, w_up_pad, b_gk, gla_norm, batch, seq):
    t = pb.shape[0]
    c = ROWS_GLA
    nc = seq // c

    def rows(b, s):
        return b * nc + s

    return pl.pallas_call(
        _gla_kernel,
        out_shape=jax.ShapeDtypeStruct((t, V_B), BF16),
        grid=(batch, nc),
        in_specs=[
            pl.BlockSpec((c, QK_B), lambda b, s: (rows(b, s), 0)),
            pl.BlockSpec((c, QK_B), lambda b, s: (rows(b, s), 1)),
            pl.BlockSpec((c, V_B), lambda b, s: (rows(b, s), 1)),
            pl.BlockSpec((c, V_B), lambda b, s: (rows(b, s), 2)),
            pl.BlockSpec((c, GK_PAD), lambda b, s: (rows(b, s), 0)),
            pl.BlockSpec((GK_PAD, QK_B), lambda b, s: (0, 0)),
            pl.BlockSpec((1, QK_B), lambda b, s: (0, 0)),
            pl.BlockSpec((1, DV_B), lambda b, s: (0, 0)),
        ],
        out_specs=pl.BlockSpec((c, V_B), lambda b, s: (rows(b, s), 0)),
        scratch_shapes=[pltpu.VMEM((H_B, DV_B, DK_B), F32)],
        compiler_params=_params("parallel", "arbitrary"),
        name="gla",
    )(pb, pb, pb, pb, lr, w_up_pad, b_gk, gla_norm)


def _merge_kernel(oa_ref, ob_ref, wa_ref, wb_ref, ga_ref, gb_ref, o_ref):
    sg_a = jax.nn.sigmoid(ga_ref[...].astype(F32))
    sg_b = jax.nn.sigmoid(gb_ref[...].astype(F32))
    ta = _dot(oa_ref[...], wa_ref[...])
    tb = _dot(ob_ref[...], wb_ref[...])
    o_ref[...] = (sg_a * ta + sg_b * tb).astype(o_ref.dtype)


def _merge(o_a, o_b, w_a, w_b, gates, *, gate_col0):
    t = o_a.shape[0]
    d = w_a.shape[1]
    tm, tn = TM_MERGE, TN_MERGE
    nj = d // tn
    assert gate_col0 % tn == 0
    g0 = gate_col0 // tn
    return pl.pallas_call(
        _merge_kernel,
        out_shape=jax.ShapeDtypeStruct((t, d), BF16),
        grid=(t // tm, nj),
        in_specs=[
            pl.BlockSpec((tm, Q_A), lambda i, j: (i, 0)),
            pl.BlockSpec((tm, V_B), lambda i, j: (i, 0)),
            pl.BlockSpec((Q_A, tn), lambda i, j: (0, j)),
            pl.BlockSpec((V_B, tn), lambda i, j: (0, j)),
            pl.BlockSpec((tm, tn), lambda i, j: (i, g0 + j)),
            pl.BlockSpec((tm, tn), lambda i, j: (i, g0 + nj + j)),
        ],
        out_specs=pl.BlockSpec((tm, tn), lambda i, j: (i, j)),
        compiler_params=_params("parallel", "parallel"),
        name="merge",
    )(o_a, o_b, w_a, w_b, gates, gates)


def _rms_residual(x, gate, y, nw):
    ms = jnp.mean(y * y, axis=-1, keepdims=True)
    return x + gate * (y * lax.rsqrt(ms + EPS) * nw)


def _out_proj_kernel(a_ref, w_ref, x_ref, g_ref, nw_ref, nw2_ref, sc2_ref, sh2_ref, o_ref, h_ref):
    for r in range(0, a_ref.shape[0], ROWS_EPILOGUE):
        rows = slice(r, r + ROWS_EPILOGUE)
        y = _dot(a_ref[rows, :], w_ref[...])
        x1 = _rms_residual(x_ref[rows, :], g_ref[...], y, nw_ref[...])
        o_ref[rows, :] = x1
        ms = jnp.mean(x1 * x1, axis=-1, keepdims=True)
        h2 = x1 * lax.rsqrt(ms + EPS) * nw2_ref[...]
        h_ref[rows, :] = (h2 * (1.0 + sc2_ref[...]) + sh2_ref[...]).astype(h_ref.dtype)


def _out_proj(a, w, x2, gate, nw, nw2, sc2, sh2, seq):
    t, k = a.shape
    d = w.shape[1]
    tm = TM_OUT
    per_seq = seq // tm
    row = pl.BlockSpec((1, d), lambda i: (0, 0))
    per_batch = pl.BlockSpec((None, 1, d), lambda i: (i // per_seq, 0, 0))
    return pl.pallas_call(
        _out_proj_kernel,
        out_shape=(jax.ShapeDtypeStruct((t, d), F32), jax.ShapeDtypeStruct((t, d), BF16)),
        grid=(t // tm,),
        in_specs=[
            pl.BlockSpec((tm, k), lambda i: (i, 0)),
            pl.BlockSpec((k, d), lambda i: (0, 0)),
            pl.BlockSpec((tm, d), lambda i: (i, 0)),
            per_batch, row, row, per_batch, per_batch,
        ],
        out_specs=(pl.BlockSpec((tm, d), lambda i: (i, 0)), pl.BlockSpec((tm, d), lambda i: (i, 0))),
        compiler_params=_params("parallel"),
        name="out_proj",
    )(a, w, x2, gate, nw, nw2, sc2, sh2)


def _ffn_kernel(h_ref, wu_ref, cwg_ref, cwv_ref, cbg_ref, cbv_ref, wd_ref, x_hbm, g_ref, nw_ref,
                o_ref, x_ref, x_sem, act0_ref, act1_ref, carry_ref, *, nj, tiles_per_seq):
    i = pl.program_id(0)
    j = pl.program_id(1)
    tm = h_ref.shape[0]
    tn = wu_ref.shape[1] // 2
    seq_start = (i % tiles_per_seq) == 0

    def x_copy():
        return pltpu.make_async_copy(x_hbm.at[pl.ds(pl.multiple_of(i * tm, tm), tm), :], x_ref, x_sem)

    def up(act_ref):
        h = h_ref[...]
        jc = jnp.minimum(j, nj - 1)
        tails = []

        def conv(u, cols, cw_ref, cb_ref):
            row = lax.broadcasted_iota(jnp.int32, u.shape, 0)
            c6 = jnp.where(seq_start, 0.0, carry_ref[jc, SUBLANES - 2:SUBLANES - 1, cols])
            c7 = jnp.where(seq_start, 0.0, carry_ref[jc, SUBLANES - 1:SUBLANES, cols])
            u1 = jnp.where(row == 0, c7, pltpu.roll(u, 1, 0))
            u2 = jnp.where(row == 0, c6, jnp.where(row == 1, c7, pltpu.roll(u, 2, 0)))
            tails.append((cols, u[tm - SUBLANES:, :]))
            return cw_ref[0:1, :] * u2 + cw_ref[1:2, :] * u1 + cw_ref[2:3, :] * u + cb_ref[...]

        u = _dot(h, wu_ref[...])
        gate = conv(u[:, :tn], slice(0, tn), cwg_ref, cbg_ref)
        val = conv(u[:, tn:], slice(tn, 2 * tn), cwv_ref, cbv_ref)
        act_ref[...] = (gate * jax.nn.sigmoid(gate) * val).astype(act_ref.dtype)
        return jc, tails

    def save_carry(jc, tails):
        for cols, tail in tails:
            carry_ref[jc, :, cols] = tail

    def down(act_ref):
        o_ref[...] += _dot(act_ref[...], wd_ref[...])

    @pl.when(j == 0)
    def _():
        x_copy().start()
        o_ref[...] = jnp.zeros_like(o_ref)
        save_carry(*up(act0_ref))

    @pl.when((j > 0) & (j < nj) & (j % 2 == 1))
    def _():
        jc, tails = up(act1_ref)
        down(act0_ref)
        save_carry(jc, tails)

    @pl.when((j > 0) & (j < nj) & (j % 2 == 0))
    def _():
        jc, tails = up(act0_ref)
        down(act1_ref)
        save_carry(jc, tails)

    @pl.when(j == nj)
    def _():
        x_copy().wait()
        down(act0_ref if (nj - 1) % 2 == 0 else act1_ref)
        for r in range(0, tm, ROWS_EPILOGUE):
            rows = slice(r, r + ROWS_EPILOGUE)
            o_ref[rows, :] = _rms_residual(x_ref[rows, :], g_ref[...], o_ref[rows, :], nw_ref[...])


def _ffn(h, w_up, conv_w, conv_b, w_down, x2, gate, nw, seq):
    t, d = h.shape
    tm, tn = TM_FFN, TN_FFN
    nj = D_FF // tn
    tiles_per_seq = seq // tm

    def up_tile(j):
        return jnp.minimum(j, nj - 1)

    def down_tile(j):
        return jnp.maximum(j - 1, 0)

    return pl.pallas_call(
        functools.partial(_ffn_kernel, nj=nj, tiles_per_seq=tiles_per_seq),
        out_shape=jax.ShapeDtypeStruct((t, d), F32),
        grid=(t // tm, nj + 1),
        in_specs=[
            pl.BlockSpec((tm, d), lambda i, j: (i, 0)),
            pl.BlockSpec((None, d, 2 * tn), lambda i, j: (up_tile(j), 0, 0)),
            pl.BlockSpec((CONV_WIDTH, tn), lambda i, j: (0, up_tile(j))),
            pl.BlockSpec((CONV_WIDTH, tn), lambda i, j: (0, up_tile(j) + nj)),
            pl.BlockSpec((1, tn), lambda i, j: (0, up_tile(j))),
            pl.BlockSpec((1, tn), lambda i, j: (0, up_tile(j) + nj)),
            pl.BlockSpec((tn, d), lambda i, j: (down_tile(j), 0)),
            pl.BlockSpec(memory_space=pl.ANY),
            pl.BlockSpec((None, 1, d), lambda i, j: (i // tiles_per_seq, 0, 0)),
            pl.BlockSpec((1, d), lambda i, j: (0, 0)),
        ],
        out_specs=pl.BlockSpec((tm, d), lambda i, j: (i, 0)),
        scratch_shapes=[
            pltpu.VMEM((tm, d), F32),
            pltpu.SemaphoreType.DMA,
            pltpu.VMEM((tm, tn), BF16),
            pltpu.VMEM((tm, tn), BF16),
            pltpu.VMEM((nj, SUBLANES, 2 * tn), F32),
        ],
        compiler_params=_params("arbitrary", "arbitrary", vmem=VMEM_LIMIT_FFN),
        name="ffn",
    )(h, w_up, conv_w, conv_w, conv_b, conv_b, w_down, x2, gate, nw)


def kernel(x, c, positions, w_mod, b_mod, mix_norm_pre, mix_norm_post, w_in, attn_sinks, w_gk_up, b_gk, gla_norm, w_branch_attn, w_branch_gla, w_out, ffn_norm_pre, ffn_norm_post, w_up, conv_w, conv_b, w_down):
    batch, seq, d = x.shape
    t = batch * seq
    depth = w_mod.shape[0]
    x2 = x.reshape(t, d)

    half = HEAD_DIM_A // 2
    inv_freq = ROPE_THETA ** (-jnp.arange(0, HEAD_DIM_A, 2, dtype=F32) / HEAD_DIM_A)
    invf_row = jnp.tile(inv_freq, LANES // half).reshape(1, LANES)
    sgn_row = jnp.tile(jnp.concatenate([-jnp.ones((half,), F32), jnp.ones((half,), F32)]),
                       LANES // HEAD_DIM_A).reshape(1, LANES)
    cos_t, sin_t = _rope_tables(positions.reshape(t, 1), invf_row, sgn_row)

    c_pad = jnp.zeros((SUBLANES, d), F32).at[:batch].set(c)

    o_qb = Q_A + 2 * KV_A
    o_lr = o_qb + 2 * QK_B + V_B
    o_og = o_lr + GK_RANK

    for l in range(depth):
        mod = _modulation(c_pad, w_mod[l], b_mod[l].reshape(1, -1))[:batch]
        sh1, sc1, g1, sh2, sc2, g2 = [m.reshape(batch, 1, d) for m in jnp.split(mod, 6, axis=-1)]

        w_in_t = w_in[l].T
        w_gk_pad = jnp.zeros((GK_PAD, QK_B), BF16).at[:GK_RANK].set(w_gk_up[l].astype(BF16))
        w_up_tiles = _retile_w_up(w_up[l])

        qkv_a, gk_lr, h = _proj_a(x2, mix_norm_pre[l].reshape(1, d), sc1, sh1, w_in_t, o_lr, cos_t, sin_t, seq)
        pb = _matmul(h, w_in_t, row0=o_qb, n=w_in_t.shape[0] - o_qb - GK_RANK, skip_at=o_lr - o_qb, skip=GK_RANK,
                     name="proj_b")
        o_a = _swa(qkv_a, attn_sinks[l], seq)
        o_b = _gla(pb, gk_lr, w_gk_pad, b_gk[l].reshape(1, -1), gla_norm[l].reshape(1, -1), batch, seq)
        merged = _merge(o_a, o_b, w_branch_attn[l].astype(BF16), w_branch_gla[l].astype(BF16), pb,
                        gate_col0=o_lr - o_qb + V_B)
        x2, h = _out_proj(merged, w_out[l].astype(BF16), x2, g1, mix_norm_post[l].reshape(1, d),
                          ffn_norm_pre[l].reshape(1, d), sc2, sh2, seq)

        x2 = _ffn(h, w_up_tiles, conv_w[l], conv_b[l].reshape(1, -1), w_down[l].astype(BF16),
                  x2, g2, ffn_norm_post[l].reshape(1, d), seq)

    return x2.reshape(batch, seq, d)
```

```python
import functools

import jax
import jax.numpy as jnp
from jax import lax
from jax.experimental import pallas as pl
from jax.experimental.pallas import tpu as pltpu

F32 = jnp.float32
BF16 = jnp.bfloat16

D_MODEL = 2048
HEAD_DIM_A = 64
HQ_A = 16
HKV_A = 4
WINDOW = 128
ROPE_THETA = 10000.0
H_B = 4
DK_B = 256
DV_B = 512
GK_RANK = 16
GK_NORMALIZER = 16.0
CHUNK = 64
D_FF = 5632
CONV_WIDTH = 3
EPS = 1e-6

Q_A = HQ_A * HEAD_DIM_A
KV_A = HKV_A * HEAD_DIM_A
QK_B = H_B * DK_B
V_B = H_B * DV_B
QKV_A = Q_A + 2 * KV_A
ROPE_COLS = Q_A + KV_A

LANES = 128
SUBLANES = 8
GK_PAD = LANES
VMEM_LIMIT = 48 * 1024 * 1024
VMEM_LIMIT_FFN = 56 * 1024 * 1024

TM_ROPE = 2048
TM_PROJ_A = 512
TM_MM, TN_MM = 1024, 1024
LR_ROWS = 16
TM_MERGE, TN_MERGE = 1024, 512
TM_OUT = 512
TM_FFN, TN_FFN = 1024, 512
ROWS_MIX = 256
ROWS_EPILOGUE = 256
TN_MOD = 1024


def _params(*sem, vmem=VMEM_LIMIT):
    return pltpu.CompilerParams(dimension_semantics=sem, vmem_limit_bytes=vmem)


def _dot(a, b):
    return jnp.dot(a, b, preferred_element_type=F32)


def _dot_nt(a, b):
    return lax.dot_general(a, b, (((1,), (1,)), ((), ())), preferred_element_type=F32)


def _dot_tn(a, b):
    return lax.dot_general(a, b, (((0,), (0,)), ((), ())), preferred_element_type=F32)


def _mod_kernel(c_ref, w_ref, b_ref, o_ref):
    c = c_ref[...]
    c_act = (c * jax.nn.sigmoid(c)).astype(BF16)
    o_ref[...] = _dot(c_act, w_ref[...].astype(BF16)) + b_ref[...]


def _modulation(c_pad, w_mod, b_mod):
    rows, d = c_pad.shape
    n = w_mod.shape[1]
    return pl.pallas_call(
        _mod_kernel,
        out_shape=jax.ShapeDtypeStruct((rows, n), F32),
        grid=(n // TN_MOD,),
        in_specs=[
            pl.BlockSpec((rows, d), lambda j: (0, 0)),
            pl.BlockSpec((d, TN_MOD), lambda j: (0, j)),
            pl.BlockSpec((1, TN_MOD), lambda j: (0, j)),
        ],
        out_specs=pl.BlockSpec((rows, TN_MOD), lambda j: (0, j)),
        compiler_params=_params("parallel"),
        name="mod",
    )(c_pad, w_mod, b_mod)


def _retile_kernel(wg_ref, wv_ref, o_ref):
    tn = wg_ref.shape[1]
    o_ref[:, :tn] = wg_ref[...].astype(o_ref.dtype)
    o_ref[:, tn:] = wv_ref[...].astype(o_ref.dtype)


def _retile_w_up(w_up):
    d = w_up.shape[0]
    tn = TN_FFN
    nj = D_FF // tn
    return pl.pallas_call(
        _retile_kernel,
        out_shape=jax.ShapeDtypeStruct((nj, d, 2 * tn), BF16),
        grid=(nj,),
        in_specs=[
            pl.BlockSpec((d, tn), lambda j: (0, j)),
            pl.BlockSpec((d, tn), lambda j: (0, j + nj)),
        ],
        out_specs=pl.BlockSpec((None, d, 2 * tn), lambda j: (j, 0, 0)),
        compiler_params=_params("parallel"),
        name="retile_w_up",
    )(w_up, w_up)


def _rope_table_kernel(pos_ref, invf_ref, sgn_ref, cos_ref, sin_ref):
    ang = pos_ref[...].astype(F32) * invf_ref[...]
    cos_ref[...] = jnp.cos(ang)
    sin_ref[...] = jnp.sin(ang) * sgn_ref[...]


def _rope_tables(pos_col, invf_row, sgn_row):
    t = pos_col.shape[0]
    return pl.pallas_call(
        _rope_table_kernel,
        out_shape=(jax.ShapeDtypeStruct((t, LANES), F32), jax.ShapeDtypeStruct((t, LANES), F32)),
        grid=(t // TM_ROPE,),
        in_specs=[
            pl.BlockSpec((TM_ROPE, 1), lambda i: (i, 0)),
            pl.BlockSpec((1, LANES), lambda i: (0, 0)),
            pl.BlockSpec((1, LANES), lambda i: (0, 0)),
        ],
        out_specs=(
            pl.BlockSpec((TM_ROPE, LANES), lambda i: (i, 0)),
            pl.BlockSpec((TM_ROPE, LANES), lambda i: (i, 0)),
        ),
        compiler_params=_params("parallel"),
        name="rope_tables",
    )(pos_col, invf_row, sgn_row)


def _proj_a_kernel(x_ref, nw_ref, sc_ref, sh_ref, w_ref, wlr_ref, cos_ref, sin_ref, qkv_ref, lr_ref, h_ref,
                   wbf_ref):
    @pl.when(pl.program_id(0) == 0)
    def _():
        wbf_ref[:QKV_A, :] = w_ref[...].astype(BF16)
        wbf_ref[QKV_A:QKV_A + LR_ROWS, :] = wlr_ref[...].astype(BF16)
        wbf_ref[QKV_A + LR_ROWS:, :] = jnp.zeros((GK_PAD - LR_ROWS, wbf_ref.shape[1]), BF16)

    half = HEAD_DIM_A // 2
    for r0 in range(0, x_ref.shape[0], ROWS_EPILOGUE):
        rows = slice(r0, r0 + ROWS_EPILOGUE)
        x = x_ref[rows, :]
        ms = jnp.mean(x * x, axis=-1, keepdims=True)
        y = x * lax.rsqrt(ms + EPS) * nw_ref[...]
        h = (y * (1.0 + sc_ref[...]) + sh_ref[...]).astype(h_ref.dtype)
        h_ref[rows, :] = h
        acc = _dot_nt(h, wbf_ref[...])
        cos = cos_ref[rows, :]
        sin = sin_ref[rows, :]
        lane = lax.broadcasted_iota(jnp.int32, cos.shape, 1)
        first_half = (lane % HEAD_DIM_A) < half
        for c in range(ROPE_COLS // LANES):
            y = acc[:, c * LANES:(c + 1) * LANES]
            partner = jnp.where(first_half, pltpu.roll(y, LANES - half, 1), pltpu.roll(y, half, 1))
            r = y * cos + partner * sin
            if c < Q_A // LANES:
                r = r * (HEAD_DIM_A ** -0.5)
            qkv_ref[rows, c * LANES:(c + 1) * LANES] = r.astype(qkv_ref.dtype)
        qkv_ref[rows, ROPE_COLS:QKV_A] = acc[:, ROPE_COLS:QKV_A].astype(qkv_ref.dtype)
        lr_ref[rows, :] = acc[:, QKV_A:QKV_A + GK_PAD].astype(lr_ref.dtype)


def _proj_a(x2, nw, sc, sh, w_in_t, lr_row0, cos_t, sin_t, seq):
    t, d = x2.shape
    tm = TM_PROJ_A
    per_seq = seq // tm
    once = pl.Buffered(1)
    per_batch = pl.BlockSpec((None, 1, d), lambda i: (i // per_seq, 0, 0))
    return pl.pallas_call(
        _proj_a_kernel,
        out_shape=(jax.ShapeDtypeStruct((t, QKV_A), BF16), jax.ShapeDtypeStruct((t, GK_PAD), BF16),
                   jax.ShapeDtypeStruct((t, d), BF16)),
        grid=(t // tm,),
        in_specs=[
            pl.BlockSpec((tm, d), lambda i: (i, 0)),
            pl.BlockSpec((1, d), lambda i: (0, 0)),
            per_batch, per_batch,
            pl.BlockSpec((QKV_A, d), lambda i: (0, 0), pipeline_mode=once),
            pl.BlockSpec((pl.Element(LR_ROWS), pl.Element(d)), lambda i: (lr_row0, 0), pipeline_mode=once),
            pl.BlockSpec((tm, LANES), lambda i: (i, 0)),
            pl.BlockSpec((tm, LANES), lambda i: (i, 0)),
        ],
        out_specs=(
            pl.BlockSpec((tm, QKV_A), lambda i: (i, 0)),
            pl.BlockSpec((tm, GK_PAD), lambda i: (i, 0)),
            pl.BlockSpec((tm, d), lambda i: (i, 0)),
        ),
        scratch_shapes=[pltpu.VMEM((QKV_A + GK_PAD, d), BF16)],
        compiler_params=_params("arbitrary"),
        name="proj_a",
    )(x2, nw, sc, sh, w_in_t, w_in_t, cos_t, sin_t)


def _mm_kernel(a_ref, wt_ref, o_ref, wbf_ref):
    @pl.when(pl.program_id(1) == 0)
    def _():
        wbf_ref[...] = wt_ref[...].astype(BF16)

    o_ref[...] = _dot_nt(a_ref[...], wbf_ref[...]).astype(o_ref.dtype)


def _matmul(a, w_t, *, row0, n, skip_at, skip, name):
    m, k = a.shape
    tm, tn = TM_MM, TN_MM
    assert row0 % SUBLANES == 0 and skip % SUBLANES == 0 and n % tn == 0 and skip_at % tn == 0

    def w_row(j):
        return pl.multiple_of(row0 + j * tn + jnp.where(j * tn >= skip_at, skip, 0), SUBLANES)

    return pl.pallas_call(
        _mm_kernel,
        out_shape=jax.ShapeDtypeStruct((m, n), BF16),
        grid=(n // tn, m // tm),
        in_specs=[
            pl.BlockSpec((tm, k), lambda j, i: (i, 0)),
            pl.BlockSpec((pl.Element(tn), pl.Element(k)), lambda j, i: (w_row(j), 0)),
        ],
        out_specs=pl.BlockSpec((tm, tn), lambda j, i: (i, j)),
        scratch_shapes=[pltpu.VMEM((tn, k), BF16)],
        compiler_params=_params("parallel", "arbitrary"),
        name=name,
    )(a, w_t)


def _swa_rows(sink_ref, q_ref, kc_ref, kp_ref, vc_ref, vp_ref, o_ref, seq_start):
    w = WINDOW
    for blk in range(q_ref.shape[0] // w):
        rows = slice(blk * w, (blk + 1) * w)
        if blk == 0:
            prev_bias = jnp.where(seq_start, -jnp.inf, 0.0)
            kp, vp = kp_ref, vp_ref
        else:
            prev_bias = 0.0
            prev_rows = slice((blk - 1) * w, blk * w)
            kp, vp = kc_ref.at[prev_rows, :], vc_ref.at[prev_rows, :]
        _swa_block(sink_ref, q_ref.at[rows, :], kc_ref.at[rows, :], kp, vc_ref.at[rows, :], vp,
                   o_ref.at[rows, :], prev_bias)


def _swa_block(sink_ref, q_ref, kc_ref, kp_ref, vc_ref, vp_ref, o_ref, prev_bias):
    w = WINDOW
    dh = HEAD_DIM_A
    slot = lax.broadcasted_iota(jnp.int32, (w, w), 0)
    qry = lax.broadcasted_iota(jnp.int32, (w, w), 1)
    from_prev = slot > qry
    lo = lax.broadcasted_iota(jnp.int32, (2 * w, LANES), 1) < dh
    zero_k = jnp.zeros((2 * w, LANES), BF16)
    zero_v = jnp.zeros((dh, 2 * w), BF16)

    v_bdts = []
    windows = []
    for pair in range(HKV_A // 2):
        ls = slice(pair * LANES, (pair + 1) * LANES)
        kt = jnp.concatenate([kp_ref[:, ls], kc_ref[:, ls]], axis=0)
        kt_r = pltpu.roll(kt.astype(F32), dh, 1).astype(BF16)
        vt_t = jnp.concatenate([vp_ref[:, ls], vc_ref[:, ls]], axis=0).astype(F32).T.astype(BF16)
        for e in range(2):
            hkv = 2 * pair + e
            k_lo, k_hi = (kt, kt_r) if e == 0 else (kt_r, kt)
            k_bd = jnp.concatenate([jnp.where(lo, k_lo, zero_k), jnp.where(lo, zero_k, k_hi)], axis=0)
            v_t = vt_t[e * dh:(e + 1) * dh, :]
            v_bdts.append(jnp.concatenate([jnp.concatenate([v_t, zero_v], axis=1),
                                           jnp.concatenate([zero_v, v_t], axis=1)], axis=0))
            q2 = jnp.concatenate([q_ref[:, (2 * hkv + t) * LANES:(2 * hkv + t + 1) * LANES] for t in range(2)],
                                 axis=0)
            s_t = _dot_nt(k_bd, q2)
            for t in range(2):
                for hh in range(2):
                    blk = s_t[2 * w * hh:2 * w * (hh + 1), t * w:(t + 1) * w]
                    windows.append(jnp.where(from_prev, blk[:w, :] + prev_bias, blk[w:, :]))
    s_all = jnp.concatenate(windows, axis=1)
    sink = jnp.concatenate([jnp.full((1, w), sink_ref[n], F32) for n in range(HQ_A)], axis=1)
    m = jnp.maximum(jnp.max(s_all, axis=0, keepdims=True), sink)
    p = jnp.exp(s_all - m)
    denom = jnp.sum(p, axis=0, keepdims=True) + jnp.exp(sink - m)
    p = p * (1.0 / denom)

    zero_p = jnp.zeros((w, w), F32)
    for hkv in range(HKV_A):
        cols = []
        for t in range(2):
            parts = []
            for hh in range(2):
                n = 4 * hkv + 2 * t + hh
                ph = p[:, n * w:(n + 1) * w]
                parts += [jnp.where(from_prev, ph, zero_p), jnp.where(from_prev, zero_p, ph)]
            cols.append(jnp.concatenate(parts, axis=0))
        probs_t = jnp.concatenate(cols, axis=1).astype(BF16)
        o_t = _dot(v_bdts[hkv], probs_t)
        for t in range(2):
            tile = 2 * hkv + t
            o_ref[:, tile * LANES:(tile + 1) * LANES] = o_t[:, t * w:(t + 1) * w].T.astype(o_ref.dtype)


def _gla_rows(q_ref, k_ref, v_ref, og_ref, lr_ref, wup_ref, bgk_ref, gn_ref, o_ref, st_ref):
    c = CHUNK
    rows_blk = q_ref.shape[0]
    nch = rows_blk // c
    mid = c // 2
    z = _dot(lr_ref[...], wup_ref[...]) + bgk_ref[...]
    g = jax.nn.log_sigmoid(z) / GK_NORMALIZER
    g1 = g.astype(BF16)
    rem = g - g1.astype(F32)
    g2 = rem.astype(BF16)
    g3 = (rem - g2.astype(F32)).astype(BF16)

    def sums(mat):
        return _dot(mat, g1) + _dot(mat, g2) + _dot(mat, g3)

    shift = c.bit_length() - 1
    r_i = lax.broadcasted_iota(jnp.int32, (rows_blk, rows_blk), 0)
    c_i = lax.broadcasted_iota(jnp.int32, (rows_blk, rows_blk), 1)
    same = (r_i >> shift) == (c_i >> shift)
    r_in = r_i & (c - 1)
    c_in = c_i & (c - 1)
    plus = same & (c_in <= r_in) & (c_in > mid)
    minus = same & (c_in > r_in) & (c_in <= mid)
    d_mid = jnp.where(plus, 1.0, 0.0) - jnp.where(minus, 1.0, 0.0)
    b_rel = sums(d_mid.astype(BF16))
    sel_rows = 2 * SUBLANES
    s_r = lax.broadcasted_iota(jnp.int32, (sel_rows, rows_blk), 0)
    s_c = lax.broadcasted_iota(jnp.int32, (sel_rows, rows_blk), 1)
    s_chunk = s_c >> shift
    pick = ((s_r == s_chunk) & ((s_c & (c - 1)) <= mid)) | (s_r == s_chunk + nch)
    b_sel = sums(jnp.where(pick, 1.0, 0.0).astype(BF16))

    q = q_ref[...].astype(F32) * (DK_B ** -0.5)
    k = k_ref[...].astype(F32)
    qs = q * jnp.exp(b_rel)
    ks = k * jnp.exp(-b_rel)
    qs_b = qs.astype(BF16)
    ks_b = ks.astype(BF16)
    gn = gn_ref[...]
    cr_i = lax.broadcasted_iota(jnp.int32, (c, c), 0)
    cc_i = lax.broadcasted_iota(jnp.int32, (c, c), 1)
    causal = cr_i >= cc_i
    for ch in range(nch):
        rows = slice(ch * c, (ch + 1) * c)
        b_mid = b_sel[ch:ch + 1, :]
        b_last = b_sel[nch + ch:nch + ch + 1, :]
        q_in = (qs[rows, :] * jnp.exp(b_mid)).astype(BF16)
        k_st = (ks[rows, :] * jnp.exp(b_last - b_mid)).astype(BF16)
        decay = jnp.exp(b_last)
        for h in range(H_B):
            sk = slice(h * DK_B, (h + 1) * DK_B)
            sv = slice(h * DV_B, (h + 1) * DV_B)
            v = v_ref[rows, sv]
            a = jnp.where(causal, _dot_nt(qs_b[rows, sk], ks_b[rows, sk]), 0.0)
            st = st_ref[h]
            o = _dot(a.astype(BF16), v) + _dot_nt(q_in[:, sk], st.astype(BF16))
            st_ref[h] = st * decay[:, sk] + _dot_tn(v, k_st[:, sk])
            ms = jnp.mean(o * o, axis=-1, keepdims=True)
            og = og_ref[rows, sv].astype(F32)
            out = (o * lax.rsqrt(ms + EPS) * gn) * (og * jax.nn.sigmoid(og))
            o_ref[rows, sv] = out.astype(o_ref.dtype)


def _mixers_kernel(sink_ref, qa_ref, kc_ref, kp_ref, vc_ref, vp_ref, q_ref, k_ref, v_ref, og_ref, lr_ref, wup_ref,
                   bgk_ref, gn_ref, oa_ref, ob_ref, st_ref):
    seq_start = pl.program_id(1) == 0

    @pl.when(seq_start)
    def _():
        st_ref[...] = jnp.zeros_like(st_ref)

    _swa_rows(sink_ref, qa_ref, kc_ref, kp_ref, vc_ref, vp_ref, oa_ref, seq_start)
    _gla_rows(q_ref, k_ref, v_ref, og_ref, lr_ref, wup_ref, bgk_ref, gn_ref, ob_ref, st_ref)


def _mixers(qkv_a, sinks, pb, lr, w_up_pad, b_gk, gla_norm, batch, seq):
    t = pb.shape[0]
    c = ROWS_MIX
    nc = seq // c
    w = WINDOW
    per_step = c // w
    k_col = Q_A // KV_A

    def rows(b, s):
        return b * nc + s

    def prev_block(b, s):
        return jnp.maximum(rows(b, s) * per_step - 1, 0)

    return pl.pallas_call(
        _mixers_kernel,
        out_shape=(jax.ShapeDtypeStruct((t, Q_A), BF16), jax.ShapeDtypeStruct((t, V_B), BF16)),
        grid=(batch, nc),
        in_specs=[
            pl.BlockSpec(memory_space=pltpu.SMEM),
            pl.BlockSpec((c, Q_A), lambda b, s: (rows(b, s), 0)),
            pl.BlockSpec((c, KV_A), lambda b, s: (rows(b, s), k_col)),
            pl.BlockSpec((w, KV_A), lambda b, s: (prev_block(b, s), k_col)),
            pl.BlockSpec((c, KV_A), lambda b, s: (rows(b, s), k_col + 1)),
            pl.BlockSpec((w, KV_A), lambda b, s: (prev_block(b, s), k_col + 1)),
            pl.BlockSpec((c, QK_B), lambda b, s: (rows(b, s), 0)),
            pl.BlockSpec((c, QK_B), lambda b, s: (rows(b, s), 1)),
            pl.BlockSpec((c, V_B), lambda b, s: (rows(b, s), 1)),
            pl.BlockSpec((c, V_B), lambda b, s: (rows(b, s), 2)),
            pl.BlockSpec((c, GK_PAD), lambda b, s: (rows(b, s), 0)),
            pl.BlockSpec((GK_PAD, QK_B), lambda b, s: (0, 0)),
            pl.BlockSpec((1, QK_B), lambda b, s: (0, 0)),
            pl.BlockSpec((1, DV_B), lambda b, s: (0, 0)),
        ],
        out_specs=(pl.BlockSpec((c, Q_A), lambda b, s: (rows(b, s), 0)),
                   pl.BlockSpec((c, V_B), lambda b, s: (rows(b, s), 0))),
        scratch_shapes=[pltpu.VMEM((H_B, DV_B, DK_B), F32)],
        compiler_params=_params("parallel", "arbitrary"),
        name="mixers",
    )(sinks, qkv_a, qkv_a, qkv_a, qkv_a, qkv_a, pb, pb, pb, pb, lr, w_up_pad, b_gk, gla_norm)


def _merge_kernel(oa_ref, ob_ref, wa_ref, wb_ref, ga_ref, gb_ref, o_ref):
    sg_a = jax.nn.sigmoid(ga_ref[...].astype(F32))
    sg_b = jax.nn.sigmoid(gb_ref[...].astype(F32))
    ta = _dot(oa_ref[...], wa_ref[...])
    tb = _dot(ob_ref[...], wb_ref[...])
    o_ref[...] = (sg_a * ta + sg_b * tb).astype(o_ref.dtype)


def _merge(o_a, o_b, w_a, w_b, gates, *, gate_col0):
    t = o_a.shape[0]
    d = w_a.shape[1]
    tm, tn = TM_MERGE, TN_MERGE
    nj = d // tn
    assert gate_col0 % tn == 0
    g0 = gate_col0 // tn
    return pl.pallas_call(
        _merge_kernel,
        out_shape=jax.ShapeDtypeStruct((t, d), BF16),
        grid=(t // tm, nj),
        in_specs=[
            pl.BlockSpec((tm, Q_A), lambda i, j: (i, 0)),
            pl.BlockSpec((tm, V_B), lambda i, j: (i, 0)),
            pl.BlockSpec((Q_A, tn), lambda i, j: (0, j)),
            pl.BlockSpec((V_B, tn), lambda i, j: (0, j)),
            pl.BlockSpec((tm, tn), lambda i, j: (i, g0 + j)),
            pl.BlockSpec((tm, tn), lambda i, j: (i, g0 + nj + j)),
        ],
        out_specs=pl.BlockSpec((tm, tn), lambda i, j: (i, j)),
        compiler_params=_params("parallel", "parallel"),
        name="merge",
    )(o_a, o_b, w_a, w_b, gates, gates)


def _rms_residual(x, gate, y, nw):
    ms = jnp.mean(y * y, axis=-1, keepdims=True)
    return x + gate * (y * lax.rsqrt(ms + EPS) * nw)


def _out_proj_kernel(a_ref, w_ref, x_ref, g_ref, nw_ref, nw2_ref, sc2_ref, sh2_ref, o_ref, h_ref):
    for r in range(0, a_ref.shape[0], ROWS_EPILOGUE):
        rows = slice(r, r + ROWS_EPILOGUE)
        y = _dot(a_ref[rows, :], w_ref[...])
        x1 = _rms_residual(x_ref[rows, :], g_ref[...], y, nw_ref[...])
        o_ref[rows, :] = x1
        ms = jnp.mean(x1 * x1, axis=-1, keepdims=True)
        h2 = x1 * lax.rsqrt(ms + EPS) * nw2_ref[...]
        h_ref[rows, :] = (h2 * (1.0 + sc2_ref[...]) + sh2_ref[...]).astype(h_ref.dtype)


def _out_proj(a, w, x2, gate, nw, nw2, sc2, sh2, seq):
    t, k = a.shape
    d = w.shape[1]
    tm = TM_OUT
    per_seq = seq // tm
    row = pl.BlockSpec((1, d), lambda i: (0, 0))
    per_batch = pl.BlockSpec((None, 1, d), lambda i: (i // per_seq, 0, 0))
    return pl.pallas_call(
        _out_proj_kernel,
        out_shape=(jax.ShapeDtypeStruct((t, d), F32), jax.ShapeDtypeStruct((t, d), BF16)),
        grid=(t // tm,),
        in_specs=[
            pl.BlockSpec((tm, k), lambda i: (i, 0)),
            pl.BlockSpec((k, d), lambda i: (0, 0)),
            pl.BlockSpec((tm, d), lambda i: (i, 0)),
            per_batch, row, row, per_batch, per_batch,
        ],
        out_specs=(pl.BlockSpec((tm, d), lambda i: (i, 0)), pl.BlockSpec((tm, d), lambda i: (i, 0))),
        compiler_params=_params("parallel"),
        name="out_proj",
    )(a, w, x2, gate, nw, nw2, sc2, sh2)


def _ffn_kernel(h_ref, wu_ref, cwg_ref, cwv_ref, cbg_ref, cbv_ref, wd_ref, x_hbm, g_ref, nw_ref,
                o_ref, x_ref, x_sem, act0_ref, act1_ref, carry_ref, *, nj, tiles_per_seq):
    i = pl.program_id(0)
    j = pl.program_id(1)
    tm = h_ref.shape[0]
    tn = wu_ref.shape[1] // 2
    seq_start = (i % tiles_per_seq) == 0

    def x_copy():
        return pltpu.make_async_copy(x_hbm.at[pl.ds(pl.multiple_of(i * tm, tm), tm), :], x_ref, x_sem)

    def up(act_ref):
        h = h_ref[...]
        jc = jnp.minimum(j, nj - 1)
        tails = []

        def conv(u, cols, cw_ref, cb_ref):
            row = lax.broadcasted_iota(jnp.int32, u.shape, 0)
            c6 = jnp.where(seq_start, 0.0, carry_ref[jc, SUBLANES - 2:SUBLANES - 1, cols])
            c7 = jnp.where(seq_start, 0.0, carry_ref[jc, SUBLANES - 1:SUBLANES, cols])
            u1 = jnp.where(row == 0, c7, pltpu.roll(u, 1, 0))
            u2 = jnp.where(row == 0, c6, jnp.where(row == 1, c7, pltpu.roll(u, 2, 0)))
            tails.append((cols, u[tm - SUBLANES:, :]))
            return cw_ref[0:1, :] * u2 + cw_ref[1:2, :] * u1 + cw_ref[2:3, :] * u + cb_ref[...]

        u = _dot(h, wu_ref[...])
        gate = conv(u[:, :tn], slice(0, tn), cwg_ref, cbg_ref)
        val = conv(u[:, tn:], slice(tn, 2 * tn), cwv_ref, cbv_ref)
        act_ref[...] = (gate * jax.nn.sigmoid(gate) * val).astype(act_ref.dtype)
        return jc, tails

    def save_carry(jc, tails):
        for cols, tail in tails:
            carry_ref[jc, :, cols] = tail

    def down(act_ref):
        o_ref[...] += _dot(act_ref[...], wd_ref[...])

    @pl.when(j == 0)
    def _():
        x_copy().start()
        o_ref[...] = jnp.zeros_like(o_ref)
        save_carry(*up(act0_ref))

    @pl.when((j > 0) & (j < nj) & (j % 2 == 1))
    def _():
        jc, tails = up(act1_ref)
        down(act0_ref)
        save_carry(jc, tails)

    @pl.when((j > 0) & (j < nj) & (j % 2 == 0))
    def _():
        jc, tails = up(act0_ref)
        down(act1_ref)
        save_carry(jc, tails)

    @pl.when(j == nj)
    def _():
        x_copy().wait()
        down(act0_ref if (nj - 1) % 2 == 0 else act1_ref)
        for r in range(0, tm, ROWS_EPILOGUE):
            rows = slice(r, r + ROWS_EPILOGUE)
            o_ref[rows, :] = _rms_residual(x_ref[rows, :], g_ref[...], o_ref[rows, :], nw_ref[...])


def _ffn(h, w_up, conv_w, conv_b, w_down, x2, gate, nw, seq):
    t, d = h.shape
    tm, tn = TM_FFN, TN_FFN
    nj = D_FF // tn
    tiles_per_seq = seq // tm

    def up_tile(j):
        return jnp.minimum(j, nj - 1)

    def down_tile(j):
        return jnp.maximum(j - 1, 0)

    return pl.pallas_call(
        functools.partial(_ffn_kernel, nj=nj, tiles_per_seq=tiles_per_seq),
        out_shape=jax.ShapeDtypeStruct((t, d), F32),
        grid=(t // tm, nj + 1),
        in_specs=[
            pl.BlockSpec((tm, d), lambda i, j: (i, 0)),
            pl.BlockSpec((None, d, 2 * tn), lambda i, j: (up_tile(j), 0, 0)),
            pl.BlockSpec((CONV_WIDTH, tn), lambda i, j: (0, up_tile(j))),
            pl.BlockSpec((CONV_WIDTH, tn), lambda i, j: (0, up_tile(j) + nj)),
            pl.BlockSpec((1, tn), lambda i, j: (0, up_tile(j))),
            pl.BlockSpec((1, tn), lambda i, j: (0, up_tile(j) + nj)),
            pl.BlockSpec((tn, d), lambda i, j: (down_tile(j), 0)),
            pl.BlockSpec(memory_space=pl.ANY),
            pl.BlockSpec((None, 1, d), lambda i, j: (i // tiles_per_seq, 0, 0)),
            pl.BlockSpec((1, d), lambda i, j: (0, 0)),
        ],
        out_specs=pl.BlockSpec((tm, d), lambda i, j: (i, 0)),
        scratch_shapes=[
            pltpu.VMEM((tm, d), F32),
            pltpu.SemaphoreType.DMA,
            pltpu.VMEM((tm, tn), BF16),
            pltpu.VMEM((tm, tn), BF16),
            pltpu.VMEM((nj, SUBLANES, 2 * tn), F32),
        ],
        compiler_params=_params("arbitrary", "arbitrary", vmem=VMEM_LIMIT_FFN),
        name="ffn",
    )(h, w_up, conv_w, conv_w, conv_b, conv_b, w_down, x2, gate, nw)


def kernel(x, c, positions, w_mod, b_mod, mix_norm_pre, mix_norm_post, w_in, attn_sinks, w_gk_up, b_gk, gla_norm, w_branch_attn, w_branch_gla, w_out, ffn_norm_pre, ffn_norm_post, w_up, conv_w, conv_b, w_down):
    batch, seq, d = x.shape
    t = batch * seq
    depth = w_mod.shape[0]
    x2 = x.reshape(t, d)

    half = HEAD_DIM_A // 2
    inv_freq = ROPE_THETA ** (-jnp.arange(0, HEAD_DIM_A, 2, dtype=F32) / HEAD_DIM_A)
    invf_row = jnp.tile(inv_freq, LANES // half).reshape(1, LANES)
    sgn_row = jnp.tile(jnp.concatenate([-jnp.ones((half,), F32), jnp.ones((half,), F32)]),
                       LANES // HEAD_DIM_A).reshape(1, LANES)
    cos_t, sin_t = _rope_tables(positions.reshape(t, 1), invf_row, sgn_row)

    c_pad = jnp.zeros((SUBLANES, d), F32).at[:batch].set(c)

    o_qb = Q_A + 2 * KV_A
    o_lr = o_qb + 2 * QK_B + V_B
    o_og = o_lr + GK_RANK

    for l in range(depth):
        mod = _modulation(c_pad, w_mod[l], b_mod[l].reshape(1, -1))[:batch]
        sh1, sc1, g1, sh2, sc2, g2 = [m.reshape(batch, 1, d) for m in jnp.split(mod, 6, axis=-1)]

        w_in_t = w_in[l].T
        w_gk_pad = jnp.zeros((GK_PAD, QK_B), BF16).at[:GK_RANK].set(w_gk_up[l].astype(BF16))
        w_up_tiles = _retile_w_up(w_up[l])

        qkv_a, gk_lr, h = _proj_a(x2, mix_norm_pre[l].reshape(1, d), sc1, sh1, w_in_t, o_lr, cos_t, sin_t, seq)
        pb = _matmul(h, w_in_t, row0=o_qb, n=w_in_t.shape[0] - o_qb - GK_RANK, skip_at=o_lr - o_qb, skip=GK_RANK,
                     name="proj_b")
        o_a, o_b = _mixers(qkv_a, attn_sinks[l], pb, gk_lr, w_gk_pad, b_gk[l].reshape(1, -1),
                           gla_norm[l].reshape(1, -1), batch, seq)
        merged = _merge(o_a, o_b, w_branch_attn[l].astype(BF16), w_branch_gla[l].astype(BF16), pb,
                        gate_col0=o_lr - o_qb + V_B)
        x2, h = _out_proj(merged, w_out[l].astype(BF16), x2, g1, mix_norm_post[l].reshape(1, d),
                          ffn_norm_pre[l].reshape(1, d), sc2, sh2, seq)

        x2 = _ffn(h, w_up_tiles, conv_w[l], conv_b[l].reshape(1, -1), w_down[l].astype(BF16),
                  x2, g2, ffn_norm_post[l].reshape(1, d), seq)

    return x2.reshape(batch, seq, d)
```

```python
import functools

import jax
import jax.numpy as jnp
from jax import lax
from jax.experimental import pallas as pl
from jax.experimental.pallas import tpu as pltpu

F32 = jnp.float32
BF16 = jnp.bfloat16

D_MODEL = 2048
HEAD_DIM_A = 64
HQ_A = 16
HKV_A = 4
WINDOW = 128
ROPE_THETA = 10000.0
H_B = 4
DK_B = 256
DV_B = 512
GK_RANK = 16
GK_NORMALIZER = 16.0
CHUNK = 64
D_FF = 5632
CONV_WIDTH = 3
EPS = 1e-6

Q_A = HQ_A * HEAD_DIM_A
KV_A = HKV_A * HEAD_DIM_A
QK_B = H_B * DK_B
V_B = H_B * DV_B
QKV_A = Q_A + 2 * KV_A
ROPE_COLS = Q_A + KV_A

LANES = 128
SUBLANES = 8
GK_PAD = LANES
VMEM_LIMIT = 48 * 1024 * 1024
VMEM_LIMIT_FFN = 56 * 1024 * 1024

TM_ROPE = 2048
TM_PROJ_A = 512
TM_MM, TN_MM = 1024, 1024
LR_ROWS = 16
TM_MERGE, TN_MERGE = 1024, 512
TM_OUT = 512
TM_FFN, TN_FFN = 1024, 512
ROWS_MIX = 256
ROWS_EPILOGUE = 256
TN_MOD = 1024


def _params(*sem, vmem=VMEM_LIMIT):
    return pltpu.CompilerParams(dimension_semantics=sem, vmem_limit_bytes=vmem)


def _dot(a, b):
    return jnp.dot(a, b, preferred_element_type=F32)


def _dot_nt(a, b):
    return lax.dot_general(a, b, (((1,), (1,)), ((), ())), preferred_element_type=F32)


def _dot_tn(a, b):
    return lax.dot_general(a, b, (((0,), (0,)), ((), ())), preferred_element_type=F32)


def _mod_kernel(c_ref, w_ref, b_ref, o_ref):
    c = c_ref[...]
    c_act = (c * jax.nn.sigmoid(c)).astype(BF16)
    o_ref[...] = _dot(c_act, w_ref[...].astype(BF16)) + b_ref[...]


def _modulation(c_pad, w_mod, b_mod):
    rows, d = c_pad.shape
    n = w_mod.shape[1]
    return pl.pallas_call(
        _mod_kernel,
        out_shape=jax.ShapeDtypeStruct((rows, n), F32),
        grid=(n // TN_MOD,),
        in_specs=[
            pl.BlockSpec((rows, d), lambda j: (0, 0)),
            pl.BlockSpec((d, TN_MOD), lambda j: (0, j)),
            pl.BlockSpec((1, TN_MOD), lambda j: (0, j)),
        ],
        out_specs=pl.BlockSpec((rows, TN_MOD), lambda j: (0, j)),
        compiler_params=_params("parallel"),
        name="mod",
    )(c_pad, w_mod, b_mod)


def _retile_kernel(wg_ref, wv_ref, o_ref):
    tn = wg_ref.shape[1]
    o_ref[:, :tn] = wg_ref[...].astype(o_ref.dtype)
    o_ref[:, tn:] = wv_ref[...].astype(o_ref.dtype)


def _retile_w_up(w_up):
    d = w_up.shape[0]
    tn = TN_FFN
    nj = D_FF // tn
    return pl.pallas_call(
        _retile_kernel,
        out_shape=jax.ShapeDtypeStruct((nj, d, 2 * tn), BF16),
        grid=(nj,),
        in_specs=[
            pl.BlockSpec((d, tn), lambda j: (0, j)),
            pl.BlockSpec((d, tn), lambda j: (0, j + nj)),
        ],
        out_specs=pl.BlockSpec((None, d, 2 * tn), lambda j: (j, 0, 0)),
        compiler_params=_params("parallel"),
        name="retile_w_up",
    )(w_up, w_up)


def _rope_table_kernel(pos_ref, invf_ref, sgn_ref, cos_ref, sin_ref):
    ang = pos_ref[...].astype(F32) * invf_ref[...]
    cos_ref[...] = jnp.cos(ang)
    sin_ref[...] = jnp.sin(ang) * sgn_ref[...]


def _rope_tables(pos_col, invf_row, sgn_row):
    t = pos_col.shape[0]
    return pl.pallas_call(
        _rope_table_kernel,
        out_shape=(jax.ShapeDtypeStruct((t, LANES), F32), jax.ShapeDtypeStruct((t, LANES), F32)),
        grid=(t // TM_ROPE,),
        in_specs=[
            pl.BlockSpec((TM_ROPE, 1), lambda i: (i, 0)),
            pl.BlockSpec((1, LANES), lambda i: (0, 0)),
            pl.BlockSpec((1, LANES), lambda i: (0, 0)),
        ],
        out_specs=(
            pl.BlockSpec((TM_ROPE, LANES), lambda i: (i, 0)),
            pl.BlockSpec((TM_ROPE, LANES), lambda i: (i, 0)),
        ),
        compiler_params=_params("parallel"),
        name="rope_tables",
    )(pos_col, invf_row, sgn_row)


def _proj_a_kernel(x_ref, nw_ref, sc_ref, sh_ref, w_ref, wlr_ref, cos_ref, sin_ref, qkv_ref, lr_ref, h_ref,
                   wbf_ref):
    @pl.when(pl.program_id(0) == 0)
    def _():
        wbf_ref[:QKV_A, :] = w_ref[...].astype(BF16)
        wbf_ref[QKV_A:QKV_A + LR_ROWS, :] = wlr_ref[...].astype(BF16)
        wbf_ref[QKV_A + LR_ROWS:, :] = jnp.zeros((GK_PAD - LR_ROWS, wbf_ref.shape[1]), BF16)

    half = HEAD_DIM_A // 2
    for r0 in range(0, x_ref.shape[0], ROWS_EPILOGUE):
        rows = slice(r0, r0 + ROWS_EPILOGUE)
        x = x_ref[rows, :]
        ms = jnp.mean(x * x, axis=-1, keepdims=True)
        y = x * lax.rsqrt(ms + EPS) * nw_ref[...]
        h = (y * (1.0 + sc_ref[...]) + sh_ref[...]).astype(h_ref.dtype)
        h_ref[rows, :] = h
        acc = _dot_nt(h, wbf_ref[...])
        cos = cos_ref[rows, :]
        sin = sin_ref[rows, :]
        lane = lax.broadcasted_iota(jnp.int32, cos.shape, 1)
        first_half = (lane % HEAD_DIM_A) < half
        for c in range(ROPE_COLS // LANES):
            y = acc[:, c * LANES:(c + 1) * LANES]
            partner = jnp.where(first_half, pltpu.roll(y, LANES - half, 1), pltpu.roll(y, half, 1))
            r = y * cos + partner * sin
            if c < Q_A // LANES:
                r = r * (HEAD_DIM_A ** -0.5)
            qkv_ref[rows, c * LANES:(c + 1) * LANES] = r.astype(qkv_ref.dtype)
        qkv_ref[rows, ROPE_COLS:QKV_A] = acc[:, ROPE_COLS:QKV_A].astype(qkv_ref.dtype)
        lr_ref[rows, :] = acc[:, QKV_A:QKV_A + GK_PAD].astype(lr_ref.dtype)


def _proj_a(x2, nw, sc, sh, w_in_t, lr_row0, cos_t, sin_t, seq):
    t, d = x2.shape
    tm = TM_PROJ_A
    per_seq = seq // tm
    once = pl.Buffered(1)
    per_batch = pl.BlockSpec((None, 1, d), lambda i: (i // per_seq, 0, 0))
    return pl.pallas_call(
        _proj_a_kernel,
        out_shape=(jax.ShapeDtypeStruct((t, QKV_A), BF16), jax.ShapeDtypeStruct((t, GK_PAD), BF16),
                   jax.ShapeDtypeStruct((t, d), BF16)),
        grid=(t // tm,),
        in_specs=[
            pl.BlockSpec((tm, d), lambda i: (i, 0)),
            pl.BlockSpec((1, d), lambda i: (0, 0)),
            per_batch, per_batch,
            pl.BlockSpec((QKV_A, d), lambda i: (0, 0), pipeline_mode=once),
            pl.BlockSpec((pl.Element(LR_ROWS), pl.Element(d)), lambda i: (lr_row0, 0), pipeline_mode=once),
            pl.BlockSpec((tm, LANES), lambda i: (i, 0)),
            pl.BlockSpec((tm, LANES), lambda i: (i, 0)),
        ],
        out_specs=(
            pl.BlockSpec((tm, QKV_A), lambda i: (i, 0)),
            pl.BlockSpec((tm, GK_PAD), lambda i: (i, 0)),
            pl.BlockSpec((tm, d), lambda i: (i, 0)),
        ),
        scratch_shapes=[pltpu.VMEM((QKV_A + GK_PAD, d), BF16)],
        compiler_params=_params("arbitrary"),
        name="proj_a",
    )(x2, nw, sc, sh, w_in_t, w_in_t, cos_t, sin_t)


def _mm_kernel(a_ref, wt_ref, o_ref, wbf_ref):
    @pl.when(pl.program_id(1) == 0)
    def _():
        wbf_ref[...] = wt_ref[...].astype(BF16)

    o_ref[...] = _dot_nt(a_ref[...], wbf_ref[...]).astype(o_ref.dtype)


def _matmul(a, w_t, *, row0, n, skip_at, skip, name):
    m, k = a.shape
    tm, tn = TM_MM, TN_MM
    assert row0 % SUBLANES == 0 and skip % SUBLANES == 0 and n % tn == 0 and skip_at % tn == 0

    def w_row(j):
        return pl.multiple_of(row0 + j * tn + jnp.where(j * tn >= skip_at, skip, 0), SUBLANES)

    return pl.pallas_call(
        _mm_kernel,
        out_shape=jax.ShapeDtypeStruct((m, n), BF16),
        grid=(n // tn, m // tm),
        in_specs=[
            pl.BlockSpec((tm, k), lambda j, i: (i, 0)),
            pl.BlockSpec((pl.Element(tn), pl.Element(k)), lambda j, i: (w_row(j), 0)),
        ],
        out_specs=pl.BlockSpec((tm, tn), lambda j, i: (i, j)),
        scratch_shapes=[pltpu.VMEM((tn, k), BF16)],
        compiler_params=_params("parallel", "arbitrary"),
        name=name,
    )(a, w_t)


def _swa_rows(sink_ref, q_ref, kc_ref, kp_ref, vc_ref, vp_ref, o_ref, seq_start):
    w = WINDOW
    for blk in range(q_ref.shape[0] // w):
        rows = slice(blk * w, (blk + 1) * w)
        if blk == 0:
            prev_bias = jnp.where(seq_start, -jnp.inf, 0.0)
            kp, vp = kp_ref, vp_ref
        else:
            prev_bias = 0.0
            prev_rows = slice((blk - 1) * w, blk * w)
            kp, vp = kc_ref.at[prev_rows, :], vc_ref.at[prev_rows, :]
        _swa_block(sink_ref, q_ref.at[rows, :], kc_ref.at[rows, :], kp, vc_ref.at[rows, :], vp,
                   o_ref.at[rows, :], prev_bias)


def _swa_block(sink_ref, q_ref, kc_ref, kp_ref, vc_ref, vp_ref, o_ref, prev_bias):
    w = WINDOW
    dh = HEAD_DIM_A
    slot = lax.broadcasted_iota(jnp.int32, (w, w), 0)
    qry = lax.broadcasted_iota(jnp.int32, (w, w), 1)
    from_prev = slot > qry
    lo = lax.broadcasted_iota(jnp.int32, (2 * w, LANES), 1) < dh
    zero_k = jnp.zeros((2 * w, LANES), BF16)
    zero_v = jnp.zeros((dh, 2 * w), BF16)

    v_bdts = []
    windows = []
    for pair in range(HKV_A // 2):
        ls = slice(pair * LANES, (pair + 1) * LANES)
        kt = jnp.concatenate([kp_ref[:, ls], kc_ref[:, ls]], axis=0)
        kt_r = pltpu.roll(kt.astype(F32), dh, 1).astype(BF16)
        vt_t = jnp.concatenate([vp_ref[:, ls], vc_ref[:, ls]], axis=0).astype(F32).T.astype(BF16)
        for e in range(2):
            hkv = 2 * pair + e
            k_lo, k_hi = (kt, kt_r) if e == 0 else (kt_r, kt)
            k_bd = jnp.concatenate([jnp.where(lo, k_lo, zero_k), jnp.where(lo, zero_k, k_hi)], axis=0)
            v_t = vt_t[e * dh:(e + 1) * dh, :]
            v_bdts.append(jnp.concatenate([jnp.concatenate([v_t, zero_v], axis=1),
                                           jnp.concatenate([zero_v, v_t], axis=1)], axis=0))
            q2 = jnp.concatenate([q_ref[:, (2 * hkv + t) * LANES:(2 * hkv + t + 1) * LANES] for t in range(2)],
                                 axis=0)
            s_t = _dot_nt(k_bd, q2)
            for t in range(2):
                for hh in range(2):
                    blk = s_t[2 * w * hh:2 * w * (hh + 1), t * w:(t + 1) * w]
                    windows.append(jnp.where(from_prev, blk[:w, :] + prev_bias, blk[w:, :]))
    s_all = jnp.concatenate(windows, axis=1)
    sink = jnp.concatenate([jnp.full((1, w), sink_ref[n], F32) for n in range(HQ_A)], axis=1)
    m = jnp.maximum(jnp.max(s_all, axis=0, keepdims=True), sink)
    p = jnp.exp(s_all - m)
    denom = jnp.sum(p, axis=0, keepdims=True) + jnp.exp(sink - m)
    p = p * (1.0 / denom)

    zero_p = jnp.zeros((w, w), F32)
    for hkv in range(HKV_A):
        cols = []
        for t in range(2):
            parts = []
            for hh in range(2):
                n = 4 * hkv + 2 * t + hh
                ph = p[:, n * w:(n + 1) * w]
                parts += [jnp.where(from_prev, ph, zero_p), jnp.where(from_prev, zero_p, ph)]
            cols.append(jnp.concatenate(parts, axis=0))
        probs_t = jnp.concatenate(cols, axis=1).astype(BF16)
        o_t = _dot(v_bdts[hkv], probs_t)
        for t in range(2):
            tile = 2 * hkv + t
            o_ref[:, tile * LANES:(tile + 1) * LANES] = o_t[:, t * w:(t + 1) * w].T.astype(o_ref.dtype)


def _gla_rows(q_ref, k_ref, v_ref, og_ref, lr_ref, wup_ref, bgk_ref, gn_ref, o_ref, st_ref):
    c = CHUNK
    rows_blk = q_ref.shape[0]
    nch = rows_blk // c
    mid = c // 2
    z = _dot(lr_ref[...], wup_ref[...]) + bgk_ref[...]
    g = jax.nn.log_sigmoid(z) / GK_NORMALIZER
    g1 = g.astype(BF16)
    rem = g - g1.astype(F32)
    g2 = rem.astype(BF16)
    g3 = (rem - g2.astype(F32)).astype(BF16)

    def sums(mat):
        return _dot(mat, g1) + _dot(mat, g2) + _dot(mat, g3)

    shift = c.bit_length() - 1
    r_i = lax.broadcasted_iota(jnp.int32, (rows_blk, rows_blk), 0)
    c_i = lax.broadcasted_iota(jnp.int32, (rows_blk, rows_blk), 1)
    same = (r_i >> shift) == (c_i >> shift)
    r_in = r_i & (c - 1)
    c_in = c_i & (c - 1)
    plus = same & (c_in <= r_in) & (c_in > mid)
    minus = same & (c_in > r_in) & (c_in <= mid)
    d_mid = jnp.where(plus, 1.0, 0.0) - jnp.where(minus, 1.0, 0.0)
    b_rel = sums(d_mid.astype(BF16))
    sel_rows = 2 * SUBLANES
    s_r = lax.broadcasted_iota(jnp.int32, (sel_rows, rows_blk), 0)
    s_c = lax.broadcasted_iota(jnp.int32, (sel_rows, rows_blk), 1)
    s_chunk = s_c >> shift
    pick = ((s_r == s_chunk) & ((s_c & (c - 1)) <= mid)) | (s_r == s_chunk + nch)
    b_sel = sums(jnp.where(pick, 1.0, 0.0).astype(BF16))

    q = q_ref[...].astype(F32) * (DK_B ** -0.5)
    k = k_ref[...].astype(F32)
    qs = q * jnp.exp(b_rel)
    ks = k * jnp.exp(-b_rel)
    qs_b = qs.astype(BF16)
    ks_b = ks.astype(BF16)
    gn = gn_ref[...]
    cr_i = lax.broadcasted_iota(jnp.int32, (c, c), 0)
    cc_i = lax.broadcasted_iota(jnp.int32, (c, c), 1)
    causal = cr_i >= cc_i
    for pair in range(nch // 2):
        c1, c2 = 2 * pair, 2 * pair + 1
        rows1 = slice(c1 * c, (c1 + 1) * c)
        rows2 = slice(c2 * c, (c2 + 1) * c)
        rows12 = slice(c1 * c, (c2 + 1) * c)
        b_mid1, b_last1 = b_sel[c1:c1 + 1, :], b_sel[nch + c1:nch + c1 + 1, :]
        b_mid2, b_last2 = b_sel[c2:c2 + 1, :], b_sel[nch + c2:nch + c2 + 1, :]
        q_in1 = qs[rows1, :] * jnp.exp(b_mid1)
        q_in2 = qs[rows2, :] * jnp.exp(b_mid2)
        k_st1 = ks[rows1, :] * jnp.exp(b_last1 - b_mid1)
        k_st2 = ks[rows2, :] * jnp.exp(b_last2 - b_mid2)
        d1 = jnp.exp(b_last1)
        d2 = jnp.exp(b_last2)
        q_in12 = jnp.concatenate([q_in1, q_in2 * d1], axis=0).astype(BF16)
        k_st12 = jnp.concatenate([k_st1 * d2, k_st2], axis=0).astype(BF16)
        q_in2 = q_in2.astype(BF16)
        k_st1 = k_st1.astype(BF16)
        decay = d1 * d2
        for h in range(H_B):
            sk = slice(h * DK_B, (h + 1) * DK_B)
            sv = slice(h * DV_B, (h + 1) * DV_B)
            v12 = v_ref[rows12, sv]
            a11 = jnp.where(causal, _dot_nt(qs_b[rows1, sk], ks_b[rows1, sk]), 0.0)
            a22 = jnp.where(causal, _dot_nt(qs_b[rows2, sk], ks_b[rows2, sk]), 0.0)
            a21 = _dot_nt(q_in2[:, sk], k_st1[:, sk])
            st = st_ref[h]
            o_inter = _dot_nt(q_in12[:, sk], st.astype(BF16))
            o1 = _dot(a11.astype(BF16), v12[:c, :]) + o_inter[:c, :]
            o2 = _dot(jnp.concatenate([a21, a22], axis=1).astype(BF16), v12) + o_inter[c:, :]
            st_ref[h] = st * decay[:, sk] + _dot_tn(v12, k_st12[:, sk])
            o = jnp.concatenate([o1, o2], axis=0)
            ms = jnp.mean(o * o, axis=-1, keepdims=True)
            og = og_ref[rows12, sv].astype(F32)
            out = (o * lax.rsqrt(ms + EPS) * gn) * (og * jax.nn.sigmoid(og))
            o_ref[rows12, sv] = out.astype(o_ref.dtype)


def _mixers_kernel(sink_ref, qa_ref, kc_ref, kp_ref, vc_ref, vp_ref, q_ref, k_ref, v_ref, og_ref, lr_ref, wup_ref,
                   bgk_ref, gn_ref, oa_ref, ob_ref, st_ref):
    seq_start = pl.program_id(1) == 0

    @pl.when(seq_start)
    def _():
        st_ref[...] = jnp.zeros_like(st_ref)

    _swa_rows(sink_ref, qa_ref, kc_ref, kp_ref, vc_ref, vp_ref, oa_ref, seq_start)
    _gla_rows(q_ref, k_ref, v_ref, og_ref, lr_ref, wup_ref, bgk_ref, gn_ref, ob_ref, st_ref)


def _mixers(qkv_a, sinks, pb, lr, w_up_pad, b_gk, gla_norm, batch, seq):
    t = pb.shape[0]
    c = ROWS_MIX
    nc = seq // c
    w = WINDOW
    per_step = c // w
    k_col = Q_A // KV_A

    def rows(b, s):
        return b * nc + s

    def prev_block(b, s):
        return jnp.maximum(rows(b, s) * per_step - 1, 0)

    return pl.pallas_call(
        _mixers_kernel,
        out_shape=(jax.ShapeDtypeStruct((t, Q_A), BF16), jax.ShapeDtypeStruct((t, V_B), BF16)),
        grid=(batch, nc),
        in_specs=[
            pl.BlockSpec(memory_space=pltpu.SMEM),
            pl.BlockSpec((c, Q_A), lambda b, s: (rows(b, s), 0)),
            pl.BlockSpec((c, KV_A), lambda b, s: (rows(b, s), k_col)),
            pl.BlockSpec((w, KV_A), lambda b, s: (prev_block(b, s), k_col)),
            pl.BlockSpec((c, KV_A), lambda b, s: (rows(b, s), k_col + 1)),
            pl.BlockSpec((w, KV_A), lambda b, s: (prev_block(b, s), k_col + 1)),
            pl.BlockSpec((c, QK_B), lambda b, s: (rows(b, s), 0)),
            pl.BlockSpec((c, QK_B), lambda b, s: (rows(b, s), 1)),
            pl.BlockSpec((c, V_B), lambda b, s: (rows(b, s), 1)),
            pl.BlockSpec((c, V_B), lambda b, s: (rows(b, s), 2)),
            pl.BlockSpec((c, GK_PAD), lambda b, s: (rows(b, s), 0)),
            pl.BlockSpec((GK_PAD, QK_B), lambda b, s: (0, 0)),
            pl.BlockSpec((1, QK_B), lambda b, s: (0, 0)),
            pl.BlockSpec((1, DV_B), lambda b, s: (0, 0)),
        ],
        out_specs=(pl.BlockSpec((c, Q_A), lambda b, s: (rows(b, s), 0)),
                   pl.BlockSpec((c, V_B), lambda b, s: (rows(b, s), 0))),
        scratch_shapes=[pltpu.VMEM((H_B, DV_B, DK_B), F32)],
        compiler_params=_params("parallel", "arbitrary"),
        name="mixers",
    )(sinks, qkv_a, qkv_a, qkv_a, qkv_a, qkv_a, pb, pb, pb, pb, lr, w_up_pad, b_gk, gla_norm)


def _merge_kernel(oa_ref, ob_ref, wa_ref, wb_ref, ga_ref, gb_ref, o_ref):
    sg_a = jax.nn.sigmoid(ga_ref[...].astype(F32))
    sg_b = jax.nn.sigmoid(gb_ref[...].astype(F32))
    ta = _dot(oa_ref[...], wa_ref[...])
    tb = _dot(ob_ref[...], wb_ref[...])
    o_ref[...] = (sg_a * ta + sg_b * tb).astype(o_ref.dtype)


def _merge(o_a, o_b, w_a, w_b, gates, *, gate_col0):
    t = o_a.shape[0]
    d = w_a.shape[1]
    tm, tn = TM_MERGE, TN_MERGE
    nj = d // tn
    assert gate_col0 % tn == 0
    g0 = gate_col0 // tn
    return pl.pallas_call(
        _merge_kernel,
        out_shape=jax.ShapeDtypeStruct((t, d), BF16),
        grid=(t // tm, nj),
        in_specs=[
            pl.BlockSpec((tm, Q_A), lambda i, j: (i, 0)),
            pl.BlockSpec((tm, V_B), lambda i, j: (i, 0)),
            pl.BlockSpec((Q_A, tn), lambda i, j: (0, j)),
            pl.BlockSpec((V_B, tn), lambda i, j: (0, j)),
            pl.BlockSpec((tm, tn), lambda i, j: (i, g0 + j)),
            pl.BlockSpec((tm, tn), lambda i, j: (i, g0 + nj + j)),
        ],
        out_specs=pl.BlockSpec((tm, tn), lambda i, j: (i, j)),
        compiler_params=_params("parallel", "parallel"),
        name="merge",
    )(o_a, o_b, w_a, w_b, gates, gates)


def _rms_residual(x, gate, y, nw):
    ms = jnp.mean(y * y, axis=-1, keepdims=True)
    return x + gate * (y * lax.rsqrt(ms + EPS) * nw)


def _out_proj_kernel(a_ref, w_ref, x_ref, g_ref, nw_ref, nw2_ref, sc2_ref, sh2_ref, o_ref, h_ref):
    for r in range(0, a_ref.shape[0], ROWS_EPILOGUE):
        rows = slice(r, r + ROWS_EPILOGUE)
        y = _dot(a_ref[rows, :], w_ref[...])
        x1 = _rms_residual(x_ref[rows, :], g_ref[...], y, nw_ref[...])
        o_ref[rows, :] = x1
        ms = jnp.mean(x1 * x1, axis=-1, keepdims=True)
        h2 = x1 * lax.rsqrt(ms + EPS) * nw2_ref[...]
        h_ref[rows, :] = (h2 * (1.0 + sc2_ref[...]) + sh2_ref[...]).astype(h_ref.dtype)


def _out_proj(a, w, x2, gate, nw, nw2, sc2, sh2, seq):
    t, k = a.shape
    d = w.shape[1]
    tm = TM_OUT
    per_seq = seq // tm
    row = pl.BlockSpec((1, d), lambda i: (0, 0))
    per_batch = pl.BlockSpec((None, 1, d), lambda i: (i // per_seq, 0, 0))
    return pl.pallas_call(
        _out_proj_kernel,
        out_shape=(jax.ShapeDtypeStruct((t, d), F32), jax.ShapeDtypeStruct((t, d), BF16)),
        grid=(t // tm,),
        in_specs=[
            pl.BlockSpec((tm, k), lambda i: (i, 0)),
            pl.BlockSpec((k, d), lambda i: (0, 0)),
            pl.BlockSpec((tm, d), lambda i: (i, 0)),
            per_batch, row, row, per_batch, per_batch,
        ],
        out_specs=(pl.BlockSpec((tm, d), lambda i: (i, 0)), pl.BlockSpec((tm, d), lambda i: (i, 0))),
        compiler_params=_params("parallel"),
        name="out_proj",
    )(a, w, x2, gate, nw, nw2, sc2, sh2)


def _ffn_kernel(h_ref, wu_ref, cwg_ref, cwv_ref, cbg_ref, cbv_ref, wd_ref, x_hbm, g_ref, nw_ref,
                o_ref, x_ref, x_sem, act0_ref, act1_ref, carry_ref, *, nj, tiles_per_seq):
    i = pl.program_id(0)
    j = pl.program_id(1)
    tm = h_ref.shape[0]
    tn = wu_ref.shape[1] // 2
    seq_start = (i % tiles_per_seq) == 0

    def x_copy():
        return pltpu.make_async_copy(x_hbm.at[pl.ds(pl.multiple_of(i * tm, tm), tm), :], x_ref, x_sem)

    def up(act_ref):
        h = h_ref[...]
        jc = jnp.minimum(j, nj - 1)
        tails = []

        def conv(u, cols, cw_ref, cb_ref):
            row = lax.broadcasted_iota(jnp.int32, u.shape, 0)
            c6 = jnp.where(seq_start, 0.0, carry_ref[jc, SUBLANES - 2:SUBLANES - 1, cols])
            c7 = jnp.where(seq_start, 0.0, carry_ref[jc, SUBLANES - 1:SUBLANES, cols])
            u1 = jnp.where(row == 0, c7, pltpu.roll(u, 1, 0))
            u2 = jnp.where(row == 0, c6, jnp.where(row == 1, c7, pltpu.roll(u, 2, 0)))
            tails.append((cols, u[tm - SUBLANES:, :]))
            return cw_ref[0:1, :] * u2 + cw_ref[1:2, :] * u1 + cw_ref[2:3, :] * u + cb_ref[...]

        u = _dot(h, wu_ref[...])
        gate = conv(u[:, :tn], slice(0, tn), cwg_ref, cbg_ref)
        val = conv(u[:, tn:], slice(tn, 2 * tn), cwv_ref, cbv_ref)
        act_ref[...] = (gate * jax.nn.sigmoid(gate) * val).astype(act_ref.dtype)
        return jc, tails

    def save_carry(jc, tails):
        for cols, tail in tails:
            carry_ref[jc, :, cols] = tail

    def down(act_ref):
        o_ref[...] += _dot(act_ref[...], wd_ref[...])

    @pl.when(j == 0)
    def _():
        x_copy().start()
        o_ref[...] = jnp.zeros_like(o_ref)
        save_carry(*up(act0_ref))

    @pl.when((j > 0) & (j < nj) & (j % 2 == 1))
    def _():
        jc, tails = up(act1_ref)
        down(act0_ref)
        save_carry(jc, tails)

    @pl.when((j > 0) & (j < nj) & (j % 2 == 0))
    def _():
        jc, tails = up(act0_ref)
        down(act1_ref)
        save_carry(jc, tails)

    @pl.when(j == nj)
    def _():
        x_copy().wait()
        down(act0_ref if (nj - 1) % 2 == 0 else act1_ref)
        for r in range(0, tm, ROWS_EPILOGUE):
            rows = slice(r, r + ROWS_EPILOGUE)
            o_ref[rows, :] = _rms_residual(x_ref[rows, :], g_ref[...], o_ref[rows, :], nw_ref[...])


def _ffn(h, w_up, conv_w, conv_b, w_down, x2, gate, nw, seq):
    t, d = h.shape
    tm, tn = TM_FFN, TN_FFN
    nj = D_FF // tn
    tiles_per_seq = seq // tm

    def up_tile(j):
        return jnp.minimum(j, nj - 1)

    def down_tile(j):
        return jnp.maximum(j - 1, 0)

    return pl.pallas_call(
        functools.partial(_ffn_kernel, nj=nj, tiles_per_seq=tiles_per_seq),
        out_shape=jax.ShapeDtypeStruct((t, d), F32),
        grid=(t // tm, nj + 1),
        in_specs=[
            pl.BlockSpec((tm, d), lambda i, j: (i, 0)),
            pl.BlockSpec((None, d, 2 * tn), lambda i, j: (up_tile(j), 0, 0)),
            pl.BlockSpec((CONV_WIDTH, tn), lambda i, j: (0, up_tile(j))),
            pl.BlockSpec((CONV_WIDTH, tn), lambda i, j: (0, up_tile(j) + nj)),
            pl.BlockSpec((1, tn), lambda i, j: (0, up_tile(j))),
            pl.BlockSpec((1, tn), lambda i, j: (0, up_tile(j) + nj)),
            pl.BlockSpec((tn, d), lambda i, j: (down_tile(j), 0)),
            pl.BlockSpec(memory_space=pl.ANY),
            pl.BlockSpec((None, 1, d), lambda i, j: (i // tiles_per_seq, 0, 0)),
            pl.BlockSpec((1, d), lambda i, j: (0, 0)),
        ],
        out_specs=pl.BlockSpec((tm, d), lambda i, j: (i, 0)),
        scratch_shapes=[
            pltpu.VMEM((tm, d), F32),
            pltpu.SemaphoreType.DMA,
            pltpu.VMEM((tm, tn), BF16),
            pltpu.VMEM((tm, tn), BF16),
            pltpu.VMEM((nj, SUBLANES, 2 * tn), F32),
        ],
        compiler_params=_params("arbitrary", "arbitrary", vmem=VMEM_LIMIT_FFN),
        name="ffn",
    )(h, w_up, conv_w, conv_w, conv_b, conv_b, w_down, x2, gate, nw)


def kernel(x, c, positions, w_mod, b_mod, mix_norm_pre, mix_norm_post, w_in, attn_sinks, w_gk_up, b_gk, gla_norm, w_branch_attn, w_branch_gla, w_out, ffn_norm_pre, ffn_norm_post, w_up, conv_w, conv_b, w_down):
    batch, seq, d = x.shape
    t = batch * seq
    depth = w_mod.shape[0]
    x2 = x.reshape(t, d)

    half = HEAD_DIM_A // 2
    inv_freq = ROPE_THETA ** (-jnp.arange(0, HEAD_DIM_A, 2, dtype=F32) / HEAD_DIM_A)
    invf_row = jnp.tile(inv_freq, LANES // half).reshape(1, LANES)
    sgn_row = jnp.tile(jnp.concatenate([-jnp.ones((half,), F32), jnp.ones((half,), F32)]),
                       LANES // HEAD_DIM_A).reshape(1, LANES)
    cos_t, sin_t = _rope_tables(positions.reshape(t, 1), invf_row, sgn_row)

    c_pad = jnp.zeros((SUBLANES, d), F32).at[:batch].set(c)

    o_qb = Q_A + 2 * KV_A
    o_lr = o_qb + 2 * QK_B + V_B
    o_og = o_lr + GK_RANK

    for l in range(depth):
        mod = _modulation(c_pad, w_mod[l], b_mod[l].reshape(1, -1))[:batch]
        sh1, sc1, g1, sh2, sc2, g2 = [m.reshape(batch, 1, d) for m in jnp.split(mod, 6, axis=-1)]

        w_in_t = w_in[l].T
        w_gk_pad = jnp.zeros((GK_PAD, QK_B), BF16).at[:GK_RANK].set(w_gk_up[l].astype(BF16))
        w_up_tiles = _retile_w_up(w_up[l])

        qkv_a, gk_lr, h = _proj_a(x2, mix_norm_pre[l].reshape(1, d), sc1, sh1, w_in_t, o_lr, cos_t, sin_t, seq)
        pb = _matmul(h, w_in_t, row0=o_qb, n=w_in_t.shape[0] - o_qb - GK_RANK, skip_at=o_lr - o_qb, skip=GK_RANK,
                     name="proj_b")
        o_a, o_b = _mixers(qkv_a, attn_sinks[l], pb, gk_lr, w_gk_pad, b_gk[l].reshape(1, -1),
                           gla_norm[l].reshape(1, -1), batch, seq)
        merged = _merge(o_a, o_b, w_branch_attn[l].astype(BF16), w_branch_gla[l].astype(BF16), pb,
                        gate_col0=o_lr - o_qb + V_B)
        x2, h = _out_proj(merged, w_out[l].astype(BF16), x2, g1, mix_norm_post[l].reshape(1, d),
                          ffn_norm_pre[l].reshape(1, d), sc2, sh2, seq)

        x2 = _ffn(h, w_up_tiles, conv_w[l], conv_b[l].reshape(1, -1), w_down[l].astype(BF16),
                  x2, g2, ffn_norm_post[l].reshape(1, d), seq)

    return x2.reshape(batch, seq, d)
```
